```python
import jax, jax.numpy as jnp
from jax import lax
import numpy as np

D_MODEL = 4096
BATCH = 1
SEQ = 16384
DEPTH = 4

f32 = jnp.float32
GRID_W = 64
CTX_LEN = 256
N_MIXERS = 3
N_LAYERS_A = (DEPTH + 2) // 3
N_LAYERS_B = (DEPTH + 1) // 3
N_LAYERS_C = DEPTH // 3
EPS = 1e-6
ROPE_THETA = 10000.0
QBLOCK = 128

ADA_RANK = 256
N_MOD = 6

D_FF = -(-8 * D_MODEL // (3 * 256)) * 256

ATT_HEAD_DIM = 128
ATT_HEADS = D_MODEL // ATT_HEAD_DIM
ATT_KV_HEADS = ATT_HEADS // 4

ML_HEADS = 8
ML_QK_DIM = D_MODEL // 2 // ML_HEADS
ML_V_DIM = D_MODEL // ML_HEADS
ML_CHUNK = 64

MLA_HEADS = D_MODEL // 128
MLA_Q_RANK = D_MODEL // 4
MLA_KV_RANK = 512
MLA_NOPE = 128
MLA_ROPE = 64
MLA_V = 128

kernel_name = "hybrid_gqa_mlstm_mla_dit_trunk"


def _rms_norm(x, g):
    xf = x.astype(f32)
    y = xf * lax.rsqrt(jnp.mean(xf * xf, axis=-1, keepdims=True) + EPS)
    return (y * g.astype(f32)).astype(x.dtype)


def _modulate(h, shift, scale):
    return h * (1 + scale) + shift


def _ada(cond, w_down, w_up, b):
    return (jax.nn.silu(cond) @ w_down) @ w_up + b


def _axial_rope_tables(rows, rot_dim):
    row = jnp.repeat(jnp.arange(rows, dtype=f32), GRID_W)
    col = jnp.tile(jnp.arange(GRID_W, dtype=f32), rows)
    n_freq = rot_dim // 4
    inv_freq = ROPE_THETA ** (-jnp.arange(n_freq, dtype=f32) / n_freq)
    ang = jnp.concatenate([row[:, None] * inv_freq, col[:, None] * inv_freq], axis=-1)
    return jnp.cos(ang), jnp.sin(ang)


def _apply_rope(x, cos, sin):
    xf = x.astype(f32)
    x1, x2 = jnp.split(xf, 2, axis=-1)
    c = cos[None, :, None, :]
    s = sin[None, :, None, :]
    return jnp.concatenate([x1 * c - x2 * s, x2 * c + x1 * s], axis=-1).astype(x.dtype)


def _block_attention(q, k, v):
    B, N, H, Dq = q.shape
    Hk = k.shape[2]
    G = H // Hk
    scale = Dq ** -0.5
    nb = N // QBLOCK
    qb = q.reshape(B, nb, QBLOCK, Hk, G, Dq).transpose(1, 0, 2, 3, 4, 5)

    def one(qblk):
        s = jnp.einsum('bqhgd,bkhd->bhgqk', qblk, k).astype(f32) * scale
        p = jax.nn.softmax(s, axis=-1).astype(v.dtype)
        return jnp.einsum('bhgqk,bkhe->bqhge', p, v)

    o = lax.map(one, qb)
    return o.transpose(1, 0, 2, 3, 4, 5).reshape(B, N, H, v.shape[-1])


def _gqa_mixer(hx, hc, w_in, q_gain, k_gain, w_out, cos, sin, need_ctx):
    def project(h):
        B, T, _ = h.shape
        nq, nk = ATT_HEADS * ATT_HEAD_DIM, ATT_KV_HEADS * ATT_HEAD_DIM
        q, k, v = jnp.split(h @ w_in, [nq, nq + nk], axis=-1)
        q = _rms_norm(q.reshape(B, T, ATT_HEADS, ATT_HEAD_DIM), q_gain)
        k = _rms_norm(k.reshape(B, T, ATT_KV_HEADS, ATT_HEAD_DIM), k_gain)
        v = v.reshape(B, T, ATT_KV_HEADS, ATT_HEAD_DIM)
        return q, k, v

    qx, kx, vx = project(hx)
    qc, kc, vc = project(hc)
    qx = _apply_rope(qx, cos, sin)
    kx = _apply_rope(kx, cos, sin)
    k_all = jnp.concatenate([kx, kc], axis=1)
    v_all = jnp.concatenate([vx, vc], axis=1)
    B, N = hx.shape[:2]
    out_x = _block_attention(qx, k_all, v_all).reshape(B, N, -1) @ w_out
    out_c = None
    if need_ctx:
        out_c = _block_attention(qc, kc, vc).reshape(B, hc.shape[1], -1) @ w_out
    return out_x, out_c


def _mla_mixer(hx, hc, w_in, q_gain, kv_gain, w_qb, w_kvb, w_out, cos, sin, need_ctx):
    def project(h, rotate):
        B, T, _ = h.shape
        cq, ckv, kr = jnp.split(h @ w_in, [MLA_Q_RANK, MLA_Q_RANK + MLA_KV_RANK], axis=-1)
        q = (_rms_norm(cq, q_gain) @ w_qb).reshape(B, T, MLA_HEADS, MLA_NOPE + MLA_ROPE)
        kv = (_rms_norm(ckv, kv_gain) @ w_kvb).reshape(B, T, MLA_HEADS, MLA_NOPE + MLA_V)
        q_nope, q_rope = jnp.split(q, [MLA_NOPE], axis=-1)
        k_nope, v = jnp.split(kv, [MLA_NOPE], axis=-1)
        kr = kr[:, :, None, :]
        if rotate:
            q_rope = _apply_rope(q_rope, cos, sin)
            kr = _apply_rope(kr, cos, sin)
        k = jnp.concatenate([k_nope, jnp.broadcast_to(kr, (B, T, MLA_HEADS, MLA_ROPE))], axis=-1)
        q = jnp.concatenate([q_nope, q_rope], axis=-1)
        return q, k, v

    qx, kx, vx = project(hx, True)
    qc, kc, vc = project(hc, False)
    k_all = jnp.concatenate([kx, kc], axis=1)
    v_all = jnp.concatenate([vx, vc], axis=1)
    B, N = hx.shape[:2]
    out_x = _block_attention(qx, k_all, v_all).reshape(B, N, -1) @ w_out
    out_c = None
    if need_ctx:
        out_c = _block_attention(qc, kc, vc).reshape(B, hc.shape[1], -1) @ w_out
    return out_x, out_c


def _flip(a):
    return jnp.flip(a, axis=1)


def _mlstm_scan(q, k, v, log_i, log_f, state):
    B, T, H, _ = q.shape
    L = ML_CHUNK
    nc = T // L

    def to_chunks(a):
        a = a.reshape((B, nc, L, H) + a.shape[3:])
        return jnp.moveaxis(a, (1, 3), (0, 2))

    mask = jnp.tril(jnp.ones((L, L), dtype=bool))

    def step(carry, xs):
        C, n, m = carry
        qc, kc, vc, li, lf = xs
        b = jnp.cumsum(lf, axis=-1)
        g = b[..., -1]
        d_log = jnp.where(mask, b[..., :, None] - b[..., None, :] + li[..., None, :], -jnp.inf)
        inter = b + m[..., None]
        m_t = jnp.maximum(inter, jnp.max(d_log, axis=-1))
        w_intra = jnp.exp(d_log - m_t[..., None])
        w_inter = jnp.exp(inter - m_t)
        s = jnp.einsum('bhtd,bhsd->bhts', qc, kc) * w_intra
        num = w_inter[..., None] * jnp.einsum('bhtd,bhde->bhte', qc, C) + jnp.einsum('bhts,bhse->bhte', s, vc)
        den = w_inter * jnp.einsum('bhtd,bhd->bht', qc, n) + jnp.sum(s, axis=-1)
        h = num / jnp.maximum(jnp.abs(den), jnp.exp(-m_t))[..., None]
        a = g[..., None] - b + li
        m_new = jnp.maximum(g + m, jnp.max(a, axis=-1))
        w_s = jnp.exp(a - m_new[..., None])
        decay = jnp.exp(g + m - m_new)
        kw = kc * w_s[..., None]
        C_new = decay[..., None, None] * C + jnp.einsum('bhsd,bhse->bhde', kw, vc)
        n_new = decay[..., None] * n + jnp.sum(kw, axis=2)
        return (C_new, n_new, m_new), h

    xs = (to_chunks(q), to_chunks(k), to_chunks(v), to_chunks(log_i), to_chunks(log_f))
    state, hs = lax.scan(step, state, xs)
    h = jnp.moveaxis(hs, (0, 2), (1, 3)).reshape(B, T, H, v.shape[-1])
    return h, state


def _mlstm_mixer(hx, hc, w_in, gate_b, h_gain, w_out, need_ctx):
    nqk, nv = ML_HEADS * ML_QK_DIM, ML_HEADS * ML_V_DIM

    def project(h):
        B, T, _ = h.shape
        q, k, v, o, gates = jnp.split(h @ w_in, [nqk, 2 * nqk, 2 * nqk + nv, 2 * nqk + 2 * nv], axis=-1)
        q = q.reshape(B, T, ML_HEADS, ML_QK_DIM).astype(f32)
        k = k.reshape(B, T, ML_HEADS, ML_QK_DIM).astype(f32) * (ML_QK_DIM ** -0.5)
        v = v.reshape(B, T, ML_HEADS, ML_V_DIM).astype(f32)
        gates = (gates.astype(f32) + gate_b.astype(f32)).reshape(B, T, 4, ML_HEADS)
        fwd = (gates[:, :, 0], jax.nn.log_sigmoid(gates[:, :, 1]))
        bwd = (gates[:, :, 2], jax.nn.log_sigmoid(gates[:, :, 3]))
        return q, k, v, o, fwd, bwd

    def bidir(q, k, v, fwd, bwd, s_f, s_b):
        h_f, s_f = _mlstm_scan(q, k, v, fwd[0], fwd[1], s_f)
        h_b, s_b = _mlstm_scan(_flip(q), _flip(k), _flip(v), _flip(bwd[0]), _flip(bwd[1]), s_b)
        return h_f + _flip(h_b), s_f, s_b

    def finish(h, o, dtype):
        B, T = h.shape[:2]
        hn = h * lax.rsqrt(jnp.mean(h * h, axis=-1, keepdims=True) + EPS)
        hn = hn * h_gain.astype(f32).reshape(ML_HEADS, ML_V_DIM)
        y = hn * jax.nn.sigmoid(o.astype(f32)).reshape(B, T, ML_HEADS, ML_V_DIM)
        return y.reshape(B, T, -1).astype(dtype) @ w_out

    B = hx.shape[0]
    zero = (jnp.zeros((B, ML_HEADS, ML_QK_DIM, ML_V_DIM), f32),
            jnp.zeros((B, ML_HEADS, ML_QK_DIM), f32),
            jnp.zeros((B, ML_HEADS), f32))
    qc, kc, vc, oc, fc, bc = project(hc)
    h_c, s_f, s_b = bidir(qc, kc, vc, fc, bc, zero, zero)
    qx, kx, vx, ox, fx, bx = project(hx)
    h_x, _, _ = bidir(qx, kx, vx, fx, bx, s_f, s_b)
    out_x = finish(h_x, ox, hx.dtype)
    out_c = finish(h_c, oc, hc.dtype) if need_ctx else None
    return out_x, out_c


def _swiglu(h, w_in, w_out):
    a, b = jnp.split(h @ w_in, 2, axis=-1)
    return (jax.nn.silu(a) * b) @ w_out


def setup_inputs(seed: int = 0) -> dict:
    key = jax.random.key(seed)
    ks = iter(jax.random.split(key, 40))
    D = D_MODEL

    def nrm(shape, scale):
        return jax.random.normal(next(ks), shape, f32) * scale

    def gain(shape):
        return 1.0 + 0.02 * jax.random.normal(next(ks), shape, f32)

    n_att_in = (ATT_HEADS + 2 * ATT_KV_HEADS) * ATT_HEAD_DIM
    n_ml_in = 2 * ML_HEADS * ML_QK_DIM + 2 * ML_HEADS * ML_V_DIM + 4 * ML_HEADS
    is_forget = (jnp.arange(4) % 2 == 1)[None, :, None]
    ig = 0.1 * jax.random.normal(next(ks), (N_LAYERS_B, 4, ML_HEADS), f32)
    fg = 3.0 + 3.0 * jax.random.uniform(next(ks), (N_LAYERS_B, 4, ML_HEADS), f32)
    ml_gate_b = jnp.where(is_forget, fg, ig).reshape(N_LAYERS_B, 4 * ML_HEADS)

    return {
        "x": nrm((BATCH, SEQ, D), 1.0),
        "c": nrm((BATCH, D), 1.0),
        "ctx": nrm((BATCH, CTX_LEN, D), 1.0),
        "c_ctx": nrm((D,), 1.0),
        "ada_down": nrm((DEPTH, D, ADA_RANK), D ** -0.5),
        "ada_up": nrm((DEPTH, ADA_RANK, N_MOD * D), 0.5 * ADA_RANK ** -0.5),
        "ada_b": nrm((DEPTH, N_MOD * D), 0.02),
        "norm_mix": gain((DEPTH, D)),
        "norm_ffn": gain((DEPTH, D)),
        "ffn_in": nrm((DEPTH, D, 2 * D_FF), D ** -0.5),
        "ffn_out": nrm((DEPTH, D_FF, D), D_FF ** -0.5),
        "att_in": nrm((N_LAYERS_A, D, n_att_in), D ** -0.5),
        "att_qnorm": gain((N_LAYERS_A, ATT_HEAD_DIM)),
        "att_knorm": gain((N_LAYERS_A, ATT_HEAD_DIM)),
        "att_out": nrm((N_LAYERS_A, ATT_HEADS * ATT_HEAD_DIM, D), (ATT_HEADS * ATT_HEAD_DIM) ** -0.5),
        "ml_in": nrm((N_LAYERS_B, D, n_ml_in), D ** -0.5),
        "ml_gate_b": ml_gate_b,
        "ml_hnorm": gain((N_LAYERS_B, ML_HEADS * ML_V_DIM)),
        "ml_out": nrm((N_LAYERS_B, ML_HEADS * ML_V_DIM, D), (ML_HEADS * ML_V_DIM) ** -0.5),
        "mla_in": nrm((N_LAYERS_C, D, MLA_Q_RANK + MLA_KV_RANK + MLA_ROPE), D ** -0.5),
        "mla_qnorm": gain((N_LAYERS_C, MLA_Q_RANK)),
        "mla_kvnorm": gain((N_LAYERS_C, MLA_KV_RANK)),
        "mla_qb": nrm((N_LAYERS_C, MLA_Q_RANK, MLA_HEADS * (MLA_NOPE + MLA_ROPE)), MLA_Q_RANK ** -0.5),
        "mla_kvb": nrm((N_LAYERS_C, MLA_KV_RANK, MLA_HEADS * (MLA_NOPE + MLA_V)), MLA_KV_RANK ** -0.5),
        "mla_out": nrm((N_LAYERS_C, MLA_HEADS * MLA_V, D), (MLA_HEADS * MLA_V) ** -0.5),
        "final_norm": gain((D,)),
    }


def reference(x, c, ctx, c_ctx, ada_down, ada_up, ada_b, norm_mix, norm_ffn, ffn_in, ffn_out,
              att_in, att_qnorm, att_knorm, att_out, ml_in, ml_gate_b, ml_hnorm, ml_out,
              mla_in, mla_qnorm, mla_kvnorm, mla_qb, mla_kvb, mla_out, final_norm):
    B, N, D = x.shape
    rows = N // GRID_W
    cos_a, sin_a = _axial_rope_tables(rows, ATT_HEAD_DIM)
    cos_m, sin_m = _axial_rope_tables(rows, MLA_ROPE)
    for i in range(DEPTH):
        kind, j = i % N_MIXERS, i // N_MIXERS
        last = i == DEPTH - 1
        mod_x = _ada(c, ada_down[i], ada_up[i], ada_b[i])[:, None, :]
        mod_c = _ada(c_ctx[None], ada_down[i], ada_up[i], ada_b[i])[:, None, :]
        sx1, cx1, gx1, sx2, cx2, gx2 = jnp.split(mod_x, N_MOD, axis=-1)
        sc1, cc1, gc1, sc2, cc2, gc2 = jnp.split(mod_c, N_MOD, axis=-1)
        hx = _modulate(_rms_norm(x, norm_mix[i]), sx1, cx1)
        hc = _modulate(_rms_norm(ctx, norm_mix[i]), sc1, cc1)
        if kind == 0:
            ox, oc = _gqa_mixer(hx, hc, att_in[j], att_qnorm[j], att_knorm[j], att_out[j],
                                cos_a, sin_a, not last)
        elif kind == 1:
            ox, oc = _mlstm_mixer(hx, hc, ml_in[j], ml_gate_b[j], ml_hnorm[j], ml_out[j], not last)
        else:
            ox, oc = _mla_mixer(hx, hc, mla_in[j], mla_qnorm[j], mla_kvnorm[j], mla_qb[j], mla_kvb[j],
                                mla_out[j], cos_m, sin_m, not last)
        x = x + gx1 * ox
        x = x + gx2 * _swiglu(_modulate(_rms_norm(x, norm_ffn[i]), sx2, cx2), ffn_in[i], ffn_out[i])
        if not last:
            ctx = ctx + gc1 * oc
            ctx = ctx + gc2 * _swiglu(_modulate(_rms_norm(ctx, norm_ffn[i]), sc2, cc2), ffn_in[i], ffn_out[i])
    return _rms_norm(x, final_norm)
```

```python
import functools
import math

import jax
import jax.numpy as jnp
from jax import lax
from jax.experimental import pallas as pl
from jax.experimental.pallas import tpu as pltpu

F32 = jnp.float32
BF16 = jnp.bfloat16

D_MODEL = 4096
GRID_W = 64
EPS = 1e-6
ROPE_THETA = 10000.0
ADA_RANK = 256
N_MOD = 6
D_FF = -(-8 * D_MODEL // (3 * 256)) * 256
ATT_HEAD_DIM = 128
ATT_HEADS = D_MODEL // ATT_HEAD_DIM
ATT_KV_HEADS = ATT_HEADS // 4
ML_HEADS = 8
ML_QK_DIM = D_MODEL // 2 // ML_HEADS
ML_V_DIM = D_MODEL // ML_HEADS
MLA_HEADS = D_MODEL // 128
MLA_Q_RANK = D_MODEL // 4
MLA_KV_RANK = 512
MLA_NOPE = 128
MLA_ROPE = 64
MLA_V = 128

LANES = 128
VMEM_LIMIT = 56 * 1024 * 1024
LOG2E = 1.4426950408889634
NEG_INF = float("-inf")
ROW_TILE = 256
ML_L = 256
ML_EXT = ML_V_DIM + LANES


def _params(*sem):
    return pltpu.CompilerParams(dimension_semantics=sem, vmem_limit_bytes=VMEM_LIMIT)


def _pick(n, cands):
    for c in cands:
        if n % c == 0:
            return c
    raise ValueError(f"no tile for {n} in {cands}")


def _ada_kernel(cond_ref, wd_ref, wu_ref, b_ref, o_ref):
    c = cond_ref[...]
    a = (c * jax.nn.sigmoid(c)).astype(BF16)
    t = jnp.dot(a, wd_ref[...].astype(BF16), preferred_element_type=F32)
    o_ref[...] = jnp.dot(t.astype(BF16), wu_ref[...].astype(BF16), preferred_element_type=F32) + b_ref[...]


def _ada_all(cond, ada_down, ada_up, ada_b):
    depth, d, r = ada_down.shape
    n6 = ada_up.shape[-1]
    tn = 2048
    return pl.pallas_call(
        _ada_kernel,
        grid=(depth, n6 // tn),
        in_specs=[
            pl.BlockSpec((8, d), lambda l, j: (0, 0)),
            pl.BlockSpec((None, d, r), lambda l, j: (l, 0, 0)),
            pl.BlockSpec((None, r, tn), lambda l, j: (l, 0, j)),
            pl.BlockSpec((None, 1, tn), lambda l, j: (l, 0, j)),
        ],
        out_specs=pl.BlockSpec((None, 8, tn), lambda l, j: (l, 0, j)),
        out_shape=jax.ShapeDtypeStruct((depth, 8, n6), F32),
        compiler_params=_params("arbitrary", "arbitrary"),
        name="ada_mod",
    )(cond, ada_down, ada_up, ada_b.reshape(depth, 1, n6))


def _norm_kernel(x_ref, g_ref, *rest, ks, kc):
    o_ref = rest[-1]
    x = x_ref[...].astype(F32)
    y = x * lax.rsqrt(jnp.mean(x * x, axis=-1, keepdims=True) + EPS) * g_ref[...]
    if ks is not None:
        mod_ref = rest[0]
        y = y * (1.0 + mod_ref[kc:kc + 1, :]) + mod_ref[ks:ks + 1, :]
    o_ref[...] = y.astype(o_ref.dtype)


def _rms_rows(x, gain, out_dtype, *, width=None, col_block=0, rows=None, mod=None, ks=None, kc=None,
              n_lat=None):
    t = x.shape[0] if rows is None else rows
    width = x.shape[1] if width is None else width
    tm = ROW_TILE
    in_specs = [pl.BlockSpec((tm, width), lambda i: (i, col_block)),
                pl.BlockSpec((1, width), lambda i: (0, 0))]
    args = [x, gain.reshape(1, width).astype(F32)]
    if mod is not None:
        nb_lat = n_lat // tm
        in_specs.append(pl.BlockSpec((None, N_MOD, width), lambda i: (jnp.where(i >= nb_lat, 1, 0), 0, 0)))
        args.append(mod)
    return pl.pallas_call(
        functools.partial(_norm_kernel, ks=ks, kc=kc),
        grid=(t // tm,),
        in_specs=in_specs,
        out_specs=pl.BlockSpec((tm, width), lambda i: (i, 0)),
        out_shape=jax.ShapeDtypeStruct((t, width), out_dtype),
        compiler_params=_params("parallel"),
        name="rms_rows",
    )(*args)


def _mm_kernel(a_ref, w_ref, *rest, has_bias):
    o_ref = rest[-1]
    acc = jnp.dot(a_ref[...], w_ref[...], preferred_element_type=F32)
    if has_bias:
        acc = acc + rest[0][...]
    o_ref[...] = acc.astype(o_ref.dtype)


def _mm(a, w, out_dtype, *, tn, bias=None):
    m, k = a.shape
    n = w.shape[1]
    tm = _pick(m, (1280, 1024, 640, 512, 256))
    in_specs = [pl.BlockSpec((tm, k), lambda i, j: (i, 0)),
                pl.BlockSpec((k, tn), lambda i, j: (0, j))]
    args = [a, w]
    if bias is not None:
        in_specs.append(pl.BlockSpec((1, tn), lambda i, j: (0, j)))
        args.append(bias)
    return pl.pallas_call(
        functools.partial(_mm_kernel, has_bias=bias is not None),
        grid=(m // tm, n // tn),
        in_specs=in_specs,
        out_specs=pl.BlockSpec((tm, tn), lambda i, j: (i, j)),
        out_shape=jax.ShapeDtypeStruct((m, n), out_dtype),
        compiler_params=_params("parallel", "arbitrary"),
        name="mm",
    )(*args)


def _mm_res_kernel(a_ref, w_ref, res_ref, gate_ref, o_ref, *scratch, nk, tm, n_lat):
    part = jnp.dot(a_ref[...], w_ref[...], preferred_element_type=F32)

    def finish(acc):
        row = pl.program_id(0) * tm + lax.broadcasted_iota(jnp.int32, (tm, 1), 0)
        gate = jnp.where(row >= n_lat, gate_ref[1:2, :], gate_ref[0:1, :])
        o_ref[...] = res_ref[...] + gate * acc

    if nk == 1:
        finish(part)
    else:
        acc_ref = scratch[0]
        kk = pl.program_id(2)

        @pl.when(kk == 0)
        def _():
            acc_ref[...] = part

        @pl.when(jnp.logical_and(kk > 0, kk < nk - 1))
        def _():
            acc_ref[...] += part

        @pl.when(kk == nk - 1)
        def _():
            finish(acc_ref[...] + part)


def _mm_res(a, w, res, gate, n_lat, *, tm, tn, tk):
    m, k = a.shape
    n = w.shape[1]
    nk = k // tk
    scratch = [pltpu.VMEM((tm, tn), F32)] if nk > 1 else []
    return pl.pallas_call(
        functools.partial(_mm_res_kernel, nk=nk, tm=tm, n_lat=n_lat),
        grid=(m // tm, n // tn, nk),
        in_specs=[pl.BlockSpec((tm, tk), lambda i, j, kk: (i, kk)),
                  pl.BlockSpec((tk, tn), lambda i, j, kk: (kk, j)),
                  pl.BlockSpec((tm, tn), lambda i, j, kk: (i, j)),
                  pl.BlockSpec((2, tn), lambda i, j, kk: (0, j))],
        out_specs=pl.BlockSpec((tm, tn), lambda i, j, kk: (i, j)),
        out_shape=jax.ShapeDtypeStruct((m, n), F32),
        scratch_shapes=scratch,
        compiler_params=_params("parallel", "arbitrary", "arbitrary"),
        name="mm_res",
    )(a, w, res, gate)


def _mm_swiglu_kernel(a_ref, wa_ref, wb_ref, o_ref):
    a = a_ref[...]
    u = jnp.dot(a, wa_ref[...], preferred_element_type=F32)
    v = jnp.dot(a, wb_ref[...], preferred_element_type=F32)
    o_ref[...] = (u * jax.nn.sigmoid(u) * v).astype(o_ref.dtype)


def _mm_swiglu(a, w, *, tn):
    m, k = a.shape
    f = w.shape[1] // 2
    nb = f // tn
    tm = _pick(m, (1280, 1024, 640, 512, 256))
    return pl.pallas_call(
        _mm_swiglu_kernel,
        grid=(m // tm, nb),
        in_specs=[pl.BlockSpec((tm, k), lambda i, j: (i, 0)),
                  pl.BlockSpec((k, tn), lambda i, j: (0, j)),
                  pl.BlockSpec((k, tn), lambda i, j: (0, j + nb))],
        out_specs=pl.BlockSpec((tm, tn), lambda i, j: (i, j)),
        out_shape=jax.ShapeDtypeStruct((m, f), BF16),
        compiler_params=_params("parallel", "arbitrary"),
        name="mm_swiglu",
    )(a, w, w)


def _flash_kernel(q_ref, *rest, groups, dq, dv, nkv, n_kparts):
    k_refs = rest[:n_kparts]
    v_ref, o_ref, m_sc, l_sc, acc_sc = rest[n_kparts:]
    kv = pl.program_id(2)

    @pl.when(kv == 0)
    def _():
        m_sc[...] = jnp.full(m_sc.shape, NEG_INF, F32)
        l_sc[...] = jnp.zeros(l_sc.shape, F32)
        acc_sc[...] = jnp.zeros(acc_sc.shape, F32)

    if n_kparts == 1:
        k = k_refs[0][...]
    else:
        k = jnp.concatenate([r[...] for r in k_refs], axis=-1)
    v = v_ref[...]
    for g in range(groups):
        q = q_ref[:, g * dq:(g + 1) * dq]
        s = lax.dot_general(q, k, (((1,), (1,)), ((), ())), preferred_element_type=F32)
        m_prev = m_sc[g]
        m_new = jnp.maximum(m_prev, jnp.max(s, axis=-1, keepdims=True))
        alpha = jnp.exp2(m_prev - m_new)
        p = jnp.exp2(s - m_new)
        l_sc[g] = alpha * l_sc[g] + jnp.sum(p, axis=-1, keepdims=True)
        acc_sc[g] = alpha * acc_sc[g] + jnp.dot(p.astype(BF16), v, preferred_element_type=F32)
        m_sc[g] = m_new

    @pl.when(kv == nkv - 1)
    def _():
        for g in range(groups):
            o_ref[:, g * dv:(g + 1) * dv] = (acc_sc[g] / l_sc[g]).astype(o_ref.dtype)


def _flash(q, v, *, n_heads_kv, groups, dq, dv, q_row0, nq, kv_row0, nkv_rows, tq, tk, k_cols, v_col0):
    nkv = nkv_rows // tk
    qb0, kb0 = q_row0 // tq, kv_row0 // tk
    in_specs = [pl.BlockSpec((tq, groups * dq), lambda h, i, j: (qb0 + i, h))]
    args = [q]
    for (arr, width, per_head, col0) in k_cols:
        if per_head:
            in_specs.append(pl.BlockSpec((tk, width), lambda h, i, j, c=col0: (kb0 + j, c + h)))
        else:
            in_specs.append(pl.BlockSpec((tk, width), lambda h, i, j, c=col0: (kb0 + j, c)))
        args.append(arr)
    in_specs.append(pl.BlockSpec((tk, dv), lambda h, i, j: (kb0 + j, v_col0 + h)))
    args.append(v)
    return pl.pallas_call(
        functools.partial(_flash_kernel, groups=groups, dq=dq, dv=dv, nkv=nkv, n_kparts=len(k_cols)),
        grid=(n_heads_kv, nq // tq, nkv),
        in_specs=in_specs,
        out_specs=pl.BlockSpec((tq, groups * dv), lambda h, i, j: (i, h)),
        out_shape=jax.ShapeDtypeStruct((nq, n_heads_kv * groups * dv), BF16),
        scratch_shapes=[pltpu.VMEM((groups, tq, 1), F32),
                        pltpu.VMEM((groups, tq, 1), F32),
                        pltpu.VMEM((groups, tq, dv), F32)],
        compiler_params=_params("parallel", "parallel", "arbitrary"),
        name="flash",
    )(*args)


def _attend(q, k_cols, v, v_col0, *, n_heads_kv, groups, dq, dv, n_lat, n_ctx):
    t = n_lat + n_ctx
    tq = _pick(n_lat, (512, 256))
    tk = _pick(t, (1280, 640, 256))
    common = dict(n_heads_kv=n_heads_kv, groups=groups, dq=dq, dv=dv, k_cols=k_cols, v_col0=v_col0)
    o_x = _flash(q, v, q_row0=0, nq=n_lat, kv_row0=0, nkv_rows=t, tq=tq, tk=tk, **common)
    o_c = _flash(q, v, q_row0=n_lat, nq=n_ctx, kv_row0=n_lat, nkv_rows=n_ctx, tq=n_ctx, tk=n_ctx,
                 **common)
    return jnp.concatenate([o_x, o_c], axis=0)


def _gqa_prep_kernel(qkv_ref, qg_ref, kg_ref, cos_ref, sin_ref, q_ref, k_ref, v_ref, *, qscale):
    cos = cos_ref[...]
    sin = sin_ref[...]
    hd = ATT_HEAD_DIM

    def head(col, gain):
        x = qkv_ref[:, col:col + hd]
        y = x * lax.rsqrt(jnp.mean(x * x, axis=-1, keepdims=True) + EPS) * gain
        return y * cos + pltpu.roll(y, hd // 2, 1) * sin

    for h in range(ATT_HEADS):
        q_ref[:, h * hd:(h + 1) * hd] = (head(h * hd, qg_ref[...]) * qscale).astype(BF16)
    k0 = ATT_HEADS * hd
    for h in range(ATT_KV_HEADS):
        k_ref[:, h * hd:(h + 1) * hd] = head(k0 + h * hd, kg_ref[...]).astype(BF16)
    v0 = k0 + ATT_KV_HEADS * hd
    v_ref[...] = qkv_ref[:, v0:v0 + ATT_KV_HEADS * hd].astype(BF16)


def _gqa_prep(qkv, q_gain, k_gain, cos, sin):
    t = qkv.shape[0]
    tm = ROW_TILE
    nq, nk = ATT_HEADS * ATT_HEAD_DIM, ATT_KV_HEADS * ATT_HEAD_DIM
    qscale = ATT_HEAD_DIM ** -0.5 * LOG2E
    return pl.pallas_call(
        functools.partial(_gqa_prep_kernel, qscale=qscale),
        grid=(t // tm,),
        in_specs=[pl.BlockSpec((tm, nq + 2 * nk), lambda i: (i, 0)),
                  pl.BlockSpec((1, ATT_HEAD_DIM), lambda i: (0, 0)),
                  pl.BlockSpec((1, ATT_HEAD_DIM), lambda i: (0, 0)),
                  pl.BlockSpec((tm, ATT_HEAD_DIM), lambda i: (i, 0)),
                  pl.BlockSpec((tm, ATT_HEAD_DIM), lambda i: (i, 0))],
        out_specs=[pl.BlockSpec((tm, nq), lambda i: (i, 0)),
                   pl.BlockSpec((tm, nk), lambda i: (i, 0)),
                   pl.BlockSpec((tm, nk), lambda i: (i, 0))],
        out_shape=[jax.ShapeDtypeStruct((t, nq), BF16),
                   jax.ShapeDtypeStruct((t, nk), BF16),
                   jax.ShapeDtypeStruct((t, nk), BF16)],
        compiler_params=_params("parallel"),
        name="gqa_prep",
    )(qkv, q_gain.reshape(1, -1), k_gain.reshape(1, -1), cos, sin)


def _gqa_mixer(h, w_in, q_gain, k_gain, cos, sin, n_lat, n_ctx):
    qkv = _mm(h, w_in.astype(BF16), F32, tn=512)
    q, k, v = _gqa_prep(qkv, q_gain, k_gain, cos, sin)
    return _attend(q, [(k, ATT_HEAD_DIM, True, 0)], v, 0, n_heads_kv=ATT_KV_HEADS,
                   groups=ATT_HEADS // ATT_KV_HEADS, dq=ATT_HEAD_DIM, dv=ATT_HEAD_DIM,
                   n_lat=n_lat, n_ctx=n_ctx)


def _swap_halves_64(x):
    lane = lax.broadcasted_iota(jnp.int32, x.shape, 1)
    return jnp.where(lane % MLA_ROPE < MLA_ROPE // 2, pltpu.roll(x, LANES - MLA_ROPE // 2, 1),
                     pltpu.roll(x, MLA_ROPE // 2, 1))


def _mla_prep_kernel(qn_ref, qr_ref, kr_ref, cos_ref, sin_ref, qf_ref, kk_ref, *, qscale):
    cos = cos_ref[...]
    sin = sin_ref[...]
    lane = lax.broadcasted_iota(jnp.int32, cos.shape, 1)
    low = lane < MLA_ROPE
    wq = MLA_NOPE + LANES
    for h in range(MLA_HEADS):
        qf_ref[:, h * wq:h * wq + MLA_NOPE] = (qn_ref[:, h * MLA_NOPE:(h + 1) * MLA_NOPE] * qscale).astype(BF16)
    for j in range(MLA_HEADS // 2):
        x = qr_ref[:, j * LANES:(j + 1) * LANES]
        y = (x * cos + _swap_halves_64(x) * sin) * qscale
        qf_ref[:, (2 * j) * wq + MLA_NOPE:(2 * j + 1) * wq] = jnp.where(low, y, 0.0).astype(BF16)
        qf_ref[:, (2 * j + 1) * wq + MLA_NOPE:(2 * j + 2) * wq] = jnp.where(low, 0.0, y).astype(BF16)
    kr = jnp.where(low, kr_ref[...], 0.0)
    ky = kr * cos + _swap_halves_64(kr) * sin
    kk_ref[...] = (ky + pltpu.roll(ky, MLA_ROPE, 1)).astype(BF16)


def _mla_prep(qp, t_in, cos, sin):
    t = qp.shape[0]
    tm = ROW_TILE
    nn, nr = MLA_HEADS * MLA_NOPE, MLA_HEADS * MLA_ROPE
    qscale = (MLA_NOPE + MLA_ROPE) ** -0.5 * LOG2E
    kr_block = (MLA_Q_RANK + MLA_KV_RANK) // LANES
    return pl.pallas_call(
        functools.partial(_mla_prep_kernel, qscale=qscale),
        grid=(t // tm,),
        in_specs=[pl.BlockSpec((tm, nn), lambda i: (i, 0)),
                  pl.BlockSpec((tm, nr), lambda i: (i, nn // nr)),
                  pl.BlockSpec((tm, LANES), lambda i: (i, kr_block)),
                  pl.BlockSpec((tm, LANES), lambda i: (i, 0)),
                  pl.BlockSpec((tm, LANES), lambda i: (i, 0))],
        out_specs=[pl.BlockSpec((tm, MLA_HEADS * (MLA_NOPE + LANES)), lambda i: (i, 0)),
                   pl.BlockSpec((tm, LANES), lambda i: (i, 0))],
        out_shape=[jax.ShapeDtypeStruct((t, MLA_HEADS * (MLA_NOPE + LANES)), BF16),
                   jax.ShapeDtypeStruct((t, LANES), BF16)],
        compiler_params=_params("parallel"),
        name="mla_prep",
    )(qp, qp, t_in, cos, sin)


def _mla_mixer(h, w_in, q_gain, kv_gain, w_qb, w_kvb, cos, sin, n_lat, n_ctx):
    pad = -w_in.shape[1] % 512
    t_in = _mm(h, jnp.pad(w_in, ((0, 0), (0, pad))).astype(BF16), F32, tn=512)
    cq = _rms_rows(t_in, q_gain, BF16, width=MLA_Q_RANK, col_block=0)
    ckv = _rms_rows(t_in, kv_gain, BF16, width=MLA_KV_RANK, col_block=MLA_Q_RANK // MLA_KV_RANK)
    wq = w_qb.reshape(MLA_Q_RANK, MLA_HEADS, MLA_NOPE + MLA_ROPE)
    wq = jnp.concatenate([wq[:, :, :MLA_NOPE].reshape(MLA_Q_RANK, -1),
                          wq[:, :, MLA_NOPE:].reshape(MLA_Q_RANK, -1)], axis=1).astype(BF16)
    wkv = w_kvb.reshape(MLA_KV_RANK, MLA_HEADS, MLA_NOPE + MLA_V)
    wkv = jnp.concatenate([wkv[:, :, :MLA_NOPE].reshape(MLA_KV_RANK, -1),
                           wkv[:, :, MLA_NOPE:].reshape(MLA_KV_RANK, -1)], axis=1).astype(BF16)
    qp = _mm(cq, wq, F32, tn=512)
    kvp = _mm(ckv, wkv, BF16, tn=512)
    qf, kk = _mla_prep(qp, t_in, cos, sin)
    return _attend(qf, [(kvp, MLA_NOPE, True, 0), (kk, LANES, False, 0)], kvp, MLA_HEADS,
                   n_heads_kv=MLA_HEADS, groups=1, dq=MLA_NOPE + LANES, dv=MLA_V,
                   n_lat=n_lat, n_ctx=n_ctx)


def _log_sigmoid(x):
    return jnp.minimum(x, 0.0) - jnp.log1p(jnp.exp(-jnp.abs(x)))


def _mlstm_kernel(q_ref, k_ref, v_ref, gc_ref, gr_ref, h_ref, c_sc, m_sc, *, rev):
    L, dk, dvv = ML_L, ML_QK_DIM, ML_V_DIM
    step = pl.program_id(0)

    @pl.when(step == 0)
    def _():
        c_sc[...] = jnp.zeros(c_sc.shape, F32)
        m_sc[...] = jnp.zeros(m_sc.shape, F32)

    t_idx = lax.broadcasted_iota(jnp.int32, (L, L), 0)
    s_idx = lax.broadcasted_iota(jnp.int32, (L, L), 1)
    mask = (s_idx >= t_idx) if rev else (s_idx <= t_idx)
    tri = mask.astype(F32)
    gc = gc_ref[...]
    gr = gr_ref[...]
    b_col = jnp.dot(tri, _log_sigmoid(gc), preferred_element_type=F32, precision=lax.Precision.HIGHEST)
    b_row = lax.dot_general(_log_sigmoid(gr), tri, (((1,), (1,)), ((), ())),
                            preferred_element_type=F32, precision=lax.Precision.HIGHEST)
    last = 0 if rev else L - 1
    i_base, f_base = (2 * ML_HEADS, 3 * ML_HEADS) if rev else (0, ML_HEADS)
    lane_e = lax.broadcasted_iota(jnp.int32, (L, LANES), 1)
    ones_col = jnp.where(lane_e == 0, 1.0, 0.0).astype(BF16)

    for hh in range(ML_HEADS):
        ci, cf = i_base + hh, f_base + hh
        bc = b_col[:, cf:cf + 1]
        br = b_row[cf:cf + 1, :]
        lic = gc[:, ci:ci + 1]
        lir = gr[ci:ci + 1, :]
        g = bc[last:last + 1, :]
        m_prev = m_sc[hh][0:1, 0:1]
        q = q_ref[:, hh * dk:(hh + 1) * dk].astype(BF16)
        kf = k_ref[:, hh * dk:(hh + 1) * dk] * (dk ** -0.5)
        v_ext = jnp.concatenate([v_ref[:, hh * dvv:(hh + 1) * dvv].astype(BF16), ones_col], axis=-1)

        d_log = jnp.where(mask, bc - br + lir, NEG_INF)
        inter = bc + m_prev
        m_t = jnp.maximum(inter, jnp.max(d_log, axis=-1, keepdims=True))
        w_intra = jnp.exp(d_log - m_t)
        w_inter = jnp.exp(inter - m_t)
        s = lax.dot_general(q, kf.astype(BF16), (((1,), (1,)), ((), ())), preferred_element_type=F32) * w_intra
        c_old = c_sc[hh]
        ext = w_inter * jnp.dot(q, c_old.astype(BF16), preferred_element_type=F32) \
            + jnp.dot(s.astype(BF16), v_ext, preferred_element_type=F32)
        den = ext[:, dvv:dvv + 1]
        h_ref[:, hh * dvv:(hh + 1) * dvv] = ext[:, :dvv] / jnp.maximum(jnp.abs(den), jnp.exp(-m_t))

        a = g - bc + lic
        m_new = jnp.maximum(g + m_prev, jnp.max(a, axis=0, keepdims=True))
        w_s = jnp.exp(a - m_new)
        decay = jnp.exp(g + m_prev - m_new)
        kw = (kf * w_s).astype(BF16)
        c_sc[hh] = decay * c_old + lax.dot_general(kw, v_ext, (((0,), (0,)), ((), ())),
                                                   preferred_element_type=F32)
        m_sc[hh] = jnp.broadcast_to(m_new, m_sc.shape[1:])


def _mlstm_scan(qkvo, gates, gates_t, n_lat, n_ctx, rev):
    t = n_lat + n_ctx
    L = ML_L
    nlat, nctx = n_lat // L, n_ctx // L
    nqk, nv = ML_HEADS * ML_QK_DIM, ML_HEADS * ML_V_DIM

    def chunk(c):
        if rev:
            return nlat + nctx - 1 - c
        return jnp.where(c < nctx, nlat + c, c - nctx)

    return pl.pallas_call(
        functools.partial(_mlstm_kernel, rev=rev),
        grid=(nlat + nctx,),
        in_specs=[pl.BlockSpec((L, nqk), lambda c: (chunk(c), 0)),
                  pl.BlockSpec((L, nqk), lambda c: (chunk(c), 1)),
                  pl.BlockSpec((L, nv), lambda c: (chunk(c), (2 * nqk) // nv)),
                  pl.BlockSpec((L, LANES), lambda c: (chunk(c), 0)),
                  pl.BlockSpec((4 * ML_HEADS, L), lambda c: (0, chunk(c)))],
        out_specs=pl.BlockSpec((L, nv), lambda c: (chunk(c), 0)),
        out_shape=jax.ShapeDtypeStruct((t, nv), F32),
        scratch_shapes=[pltpu.VMEM((ML_HEADS, ML_QK_DIM, ML_EXT), F32),
                        pltpu.VMEM((ML_HEADS, 8, LANES), F32)],
        compiler_params=_params("arbitrary"),
        name="mlstm_rev" if rev else "mlstm_fwd",
    )(qkvo, qkvo, qkvo, gates, gates_t)


def _mlstm_finish_kernel(hf_ref, hb_ref, o_ref, g_ref, y_ref):
    dvv = ML_V_DIM
    for hh in range(ML_HEADS):
        sl = slice(hh * dvv, (hh + 1) * dvv)
        h = hf_ref[:, sl] + hb_ref[:, sl]
        hn = h * lax.rsqrt(jnp.mean(h * h, axis=-1, keepdims=True) + EPS) * g_ref[:, sl]
        y_ref[:, sl] = (hn * jax.nn.sigmoid(o_ref[:, sl])).astype(BF16)


def _mlstm_finish(h_f, h_b, qkvo, h_gain):
    t, nv = h_f.shape
    tm = ROW_TILE
    o_block = (2 * ML_HEADS * ML_QK_DIM + nv) // nv
    return pl.pallas_call(
        _mlstm_finish_kernel,
        grid=(t // tm,),
        in_specs=[pl.BlockSpec((tm, nv), lambda i: (i, 0)),
                  pl.BlockSpec((tm, nv), lambda i: (i, 0)),
                  pl.BlockSpec((tm, nv), lambda i: (i, o_block)),
                  pl.BlockSpec((1, nv), lambda i: (0, 0))],
        out_specs=pl.BlockSpec((tm, nv), lambda i: (i, 0)),
        out_shape=jax.ShapeDtypeStruct((t, nv), BF16),
        compiler_params=_params("parallel"),
        name="mlstm_finish",
    )(h_f, h_b, qkvo, h_gain.reshape(1, nv))


def _mlstm_mixer(h, w_in, gate_b, h_gain, n_lat, n_ctx):
    n_main = 2 * ML_HEADS * ML_QK_DIM + 2 * ML_HEADS * ML_V_DIM
    n_g = 4 * ML_HEADS
    qkvo = _mm(h, w_in[:, :n_main].astype(BF16), F32, tn=512)
    w_g = jnp.pad(w_in[:, n_main:], ((0, 0), (0, LANES - n_g))).astype(BF16)
    b_g = jnp.pad(gate_b.astype(F32), (0, LANES - n_g)).reshape(1, LANES)
    gates = _mm(h, w_g, F32, tn=LANES, bias=b_g)
    gates_t = gates[:, :n_g].T
    h_f = _mlstm_scan(qkvo, gates, gates_t, n_lat, n_ctx, False)
    h_b = _mlstm_scan(qkvo, gates, gates_t, n_lat, n_ctx, True)
    return _mlstm_finish(h_f, h_b, qkvo, h_gain)


def _rope_tables(n_lat, n_ctx, rot_dim):
    rows = n_lat // GRID_W
    row = jnp.repeat(jnp.arange(rows, dtype=F32), GRID_W)
    col = jnp.tile(jnp.arange(GRID_W, dtype=F32), rows)
    n_freq = rot_dim // 4
    inv_freq = ROPE_THETA ** (-jnp.arange(n_freq, dtype=F32) / n_freq)
    ang = jnp.concatenate([row[:, None] * inv_freq, col[:, None] * inv_freq], axis=-1)
    cos, sin = jnp.cos(ang), jnp.sin(ang)
    reps = LANES // rot_dim
    cos_t = jnp.tile(jnp.concatenate([cos, cos], axis=-1), (1, reps))
    sin_t = jnp.tile(jnp.concatenate([-sin, sin], axis=-1), (1, reps))
    cos_t = jnp.concatenate([cos_t, jnp.ones((n_ctx, LANES), F32)], axis=0)
    sin_t = jnp.concatenate([sin_t, jnp.zeros((n_ctx, LANES), F32)], axis=0)
    return cos_t, sin_t


def kernel(x, c, ctx, c_ctx, ada_down, ada_up, ada_b, norm_mix, norm_ffn, ffn_in, ffn_out, att_in, att_qnorm,
           att_knorm, att_out, ml_in, ml_gate_b, ml_hnorm, ml_out, mla_in, mla_qnorm, mla_kvnorm, mla_qb,
           mla_kvb, mla_out, final_norm):
    b, n_lat, d = x.shape
    n_ctx = ctx.shape[1]
    depth = ada_down.shape[0]
    assert b == 1 and d == D_MODEL and n_lat % ROW_TILE == 0 and n_ctx % ML_L == 0 and n_lat % ML_L == 0

    cos_a, sin_a = _rope_tables(n_lat, n_ctx, ATT_HEAD_DIM)
    cos_m, sin_m = _rope_tables(n_lat, n_ctx, MLA_ROPE)

    cond = jnp.concatenate([c.reshape(1, d), c_ctx.reshape(1, d), jnp.zeros((6, d), F32)], axis=0)
    mods = _ada_all(cond, ada_down, ada_up, ada_b).reshape(depth, 8, N_MOD, d)

    xs = jnp.concatenate([x[0], ctx[0]], axis=0)
    t = n_lat + n_ctx
    tm_out = _pick(t, (1280, 1024, 640, 512, 256))
    tm_ffn = _pick(t, (640, 512, 256))

    for i in range(depth):
        kind, j = i % 3, i // 3
        mod = mods[i, :2]
        h = _rms_rows(xs, norm_mix[i], BF16, mod=mod, ks=0, kc=1, n_lat=n_lat)
        if kind == 0:
            o = _gqa_mixer(h, att_in[j], att_qnorm[j], att_knorm[j], cos_a, sin_a, n_lat, n_ctx)
            w_o = att_out[j]
        elif kind == 1:
            o = _mlstm_mixer(h, ml_in[j], ml_gate_b[j], ml_hnorm[j], n_lat, n_ctx)
            w_o = ml_out[j]
        else:
            o = _mla_mixer(h, mla_in[j], mla_qnorm[j], mla_kvnorm[j], mla_qb[j], mla_kvb[j], cos_m, sin_m,
                           n_lat, n_ctx)
            w_o = mla_out[j]
        xs = _mm_res(o, w_o.astype(BF16), xs, mod[:, 2], n_lat, tm=tm_out, tn=512, tk=w_o.shape[0])
        h2 = _rms_rows(xs, norm_ffn[i], BF16, mod=mod, ks=3, kc=4, n_lat=n_lat)
        act = _mm_swiglu(h2, ffn_in[i].astype(BF16), tn=256)
        xs = _mm_res(act, ffn_out[i].astype(BF16), xs, mod[:, 5], n_lat, tm=tm_ffn, tn=1024, tk=D_FF // 2)

    out = _rms_rows(xs, final_norm, F32, rows=n_lat)
    return out.reshape(1, n_lat, d)
```

```python
import functools
import math

import jax
import jax.numpy as jnp
from jax import lax
from jax.experimental import pallas as pl
from jax.experimental.pallas import tpu as pltpu

F32 = jnp.float32
BF16 = jnp.bfloat16

D_MODEL = 4096
GRID_W = 64
EPS = 1e-6
ROPE_THETA = 10000.0
ADA_RANK = 256
N_MOD = 6
D_FF = -(-8 * D_MODEL // (3 * 256)) * 256
ATT_HEAD_DIM = 128
ATT_HEADS = D_MODEL // ATT_HEAD_DIM
ATT_KV_HEADS = ATT_HEADS // 4
ML_HEADS = 8
ML_QK_DIM = D_MODEL // 2 // ML_HEADS
ML_V_DIM = D_MODEL // ML_HEADS
MLA_HEADS = D_MODEL // 128
MLA_Q_RANK = D_MODEL // 4
MLA_KV_RANK = 512
MLA_NOPE = 128
MLA_ROPE = 64
MLA_V = 128

LANES = 128
VMEM_LIMIT = 56 * 1024 * 1024
LOG2E = 1.4426950408889634
NEG_INF = float("-inf")
FIXED_MARGIN = 1.02
FIXED_LIMIT = 60.0
ROW_TILE = 256
ML_L = 256
ML_EXT = ML_V_DIM + LANES


def _params(*sem):
    return pltpu.CompilerParams(dimension_semantics=sem, vmem_limit_bytes=VMEM_LIMIT)


def _pick(n, cands):
    for c in cands:
        if n % c == 0:
            return c
    raise ValueError(f"no tile for {n} in {cands}")


def _ada_kernel(cond_ref, wd_ref, wu_ref, b_ref, o_ref):
    c = cond_ref[...]
    a = (c * jax.nn.sigmoid(c)).astype(BF16)
    t = jnp.dot(a, wd_ref[...].astype(BF16), preferred_element_type=F32)
    o_ref[...] = jnp.dot(t.astype(BF16), wu_ref[...].astype(BF16), preferred_element_type=F32) + b_ref[...]


def _ada_all(cond, ada_down, ada_up, ada_b):
    depth, d, r = ada_down.shape
    n6 = ada_up.shape[-1]
    tn = 2048
    return pl.pallas_call(
        _ada_kernel,
        grid=(depth, n6 // tn),
        in_specs=[
            pl.BlockSpec((8, d), lambda l, j: (0, 0)),
            pl.BlockSpec((None, d, r), lambda l, j: (l, 0, 0)),
            pl.BlockSpec((None, r, tn), lambda l, j: (l, 0, j)),
            pl.BlockSpec((None, 1, tn), lambda l, j: (l, 0, j)),
        ],
        out_specs=pl.BlockSpec((None, 8, tn), lambda l, j: (l, 0, j)),
        out_shape=jax.ShapeDtypeStruct((depth, 8, n6), F32),
        compiler_params=_params("arbitrary", "arbitrary"),
        name="ada_mod",
    )(cond, ada_down, ada_up, ada_b.reshape(depth, 1, n6))


def _norm_kernel(x_ref, g_ref, *rest, ks, kc):
    o_ref = rest[-1]
    x = x_ref[...].astype(F32)
    y = x * lax.rsqrt(jnp.mean(x * x, axis=-1, keepdims=True) + EPS) * g_ref[...]
    if ks is not None:
        mod_ref = rest[0]
        y = y * (1.0 + mod_ref[kc:kc + 1, :]) + mod_ref[ks:ks + 1, :]
    o_ref[...] = y.astype(o_ref.dtype)


def _rms_rows(x, gain, out_dtype, *, width=None, col_block=0, rows=None, mod=None, ks=None, kc=None,
              n_lat=None):
    t = x.shape[0] if rows is None else rows
    width = x.shape[1] if width is None else width
    tm = ROW_TILE
    in_specs = [pl.BlockSpec((tm, width), lambda i: (i, col_block)),
                pl.BlockSpec((1, width), lambda i: (0, 0))]
    args = [x, gain.reshape(1, width).astype(F32)]
    if mod is not None:
        nb_lat = n_lat // tm
        in_specs.append(pl.BlockSpec((None, N_MOD, width), lambda i: (jnp.where(i >= nb_lat, 1, 0), 0, 0)))
        args.append(mod)
    return pl.pallas_call(
        functools.partial(_norm_kernel, ks=ks, kc=kc),
        grid=(t // tm,),
        in_specs=in_specs,
        out_specs=pl.BlockSpec((tm, width), lambda i: (i, 0)),
        out_shape=jax.ShapeDtypeStruct((t, width), out_dtype),
        compiler_params=_params("parallel"),
        name="rms_rows",
    )(*args)


def _mm_kernel(a_ref, w_ref, *rest, has_bias):
    o_ref = rest[-1]
    acc = jnp.dot(a_ref[...], w_ref[...], preferred_element_type=F32)
    if has_bias:
        acc = acc + rest[0][...]
    o_ref[...] = acc.astype(o_ref.dtype)


def _mm(a, w, out_dtype, *, tn, bias=None):
    m, k = a.shape
    n = w.shape[1]
    tm = _pick(m, (1280, 1024, 640, 512, 256))
    in_specs = [pl.BlockSpec((tm, k), lambda i, j: (i, 0)),
                pl.BlockSpec((k, tn), lambda i, j: (0, j))]
    args = [a, w]
    if bias is not None:
        in_specs.append(pl.BlockSpec((1, tn), lambda i, j: (0, j)))
        args.append(bias)
    return pl.pallas_call(
        functools.partial(_mm_kernel, has_bias=bias is not None),
        grid=(m // tm, n // tn),
        in_specs=in_specs,
        out_specs=pl.BlockSpec((tm, tn), lambda i, j: (i, j)),
        out_shape=jax.ShapeDtypeStruct((m, n), out_dtype),
        compiler_params=_params("parallel", "arbitrary"),
        name="mm",
    )(*args)


def _mm_res_kernel(a_ref, w_ref, res_ref, gate_ref, o_ref, *scratch, nk, tm, n_lat):
    part = jnp.dot(a_ref[...], w_ref[...], preferred_element_type=F32)

    def finish(acc):
        row = pl.program_id(0) * tm + lax.broadcasted_iota(jnp.int32, (tm, 1), 0)
        gate = jnp.where(row >= n_lat, gate_ref[1:2, :], gate_ref[0:1, :])
        o_ref[...] = res_ref[...] + gate * acc

    if nk == 1:
        finish(part)
    else:
        acc_ref = scratch[0]
        kk = pl.program_id(2)

        @pl.when(kk == 0)
        def _():
            acc_ref[...] = part

        @pl.when(jnp.logical_and(kk > 0, kk < nk - 1))
        def _():
            acc_ref[...] += part

        @pl.when(kk == nk - 1)
        def _():
            finish(acc_ref[...] + part)


def _mm_res(a, w, res, gate, n_lat, *, tm, tn, tk):
    m, k = a.shape
    n = w.shape[1]
    nk = k // tk
    scratch = [pltpu.VMEM((tm, tn), F32)] if nk > 1 else []
    return pl.pallas_call(
        functools.partial(_mm_res_kernel, nk=nk, tm=tm, n_lat=n_lat),
        grid=(m // tm, n // tn, nk),
        in_specs=[pl.BlockSpec((tm, tk), lambda i, j, kk: (i, kk)),
                  pl.BlockSpec((tk, tn), lambda i, j, kk: (kk, j)),
                  pl.BlockSpec((tm, tn), lambda i, j, kk: (i, j)),
                  pl.BlockSpec((2, tn), lambda i, j, kk: (0, j))],
        out_specs=pl.BlockSpec((tm, tn), lambda i, j, kk: (i, j)),
        out_shape=jax.ShapeDtypeStruct((m, n), F32),
        scratch_shapes=scratch,
        compiler_params=_params("parallel", "arbitrary", "arbitrary"),
        name="mm_res",
    )(a, w, res, gate)


def _mm_swiglu_kernel(a_ref, wa_ref, wb_ref, o_ref):
    a = a_ref[...]
    u = jnp.dot(a, wa_ref[...], preferred_element_type=F32)
    v = jnp.dot(a, wb_ref[...], preferred_element_type=F32)
    o_ref[...] = (u * jax.nn.sigmoid(u) * v).astype(o_ref.dtype)


def _mm_swiglu(a, w, *, tn):
    m, k = a.shape
    f = w.shape[1] // 2
    nb = f // tn
    tm = _pick(m, (1280, 1024, 640, 512, 256))
    return pl.pallas_call(
        _mm_swiglu_kernel,
        grid=(m // tm, nb),
        in_specs=[pl.BlockSpec((tm, k), lambda i, j: (i, 0)),
                  pl.BlockSpec((k, tn), lambda i, j: (0, j)),
                  pl.BlockSpec((k, tn), lambda i, j: (0, j + nb))],
        out_specs=pl.BlockSpec((tm, tn), lambda i, j: (i, j)),
        out_shape=jax.ShapeDtypeStruct((m, f), BF16),
        compiler_params=_params("parallel", "arbitrary"),
        name="mm_swiglu",
    )(a, w, w)


def _stat_update(stat_ref, head, y_bf16):
    yf = y_bf16.astype(F32)
    n2 = jnp.max(jnp.sum(yf * yf, axis=-1, keepdims=True), axis=0, keepdims=True)
    sl = slice(head * 8, (head + 1) * 8)
    stat_ref[sl, :] = jnp.maximum(stat_ref[sl, :], jnp.broadcast_to(n2, (8, LANES)))


def _flash_kernel(q_ref, *rest, groups, dq, dv, nkv, n_kparts):
    k_refs = rest[:n_kparts]
    vt_ref, qstat_ref, kstat_ref, o_ref, m_sc, l_sc, acc_sc, fixed_sc = rest[n_kparts:]
    h = pl.program_id(0)
    kv = pl.program_id(2)
    nt = (((1,), (1,)), ((), ()))

    @pl.when(kv == 0)
    def _():
        kmax2 = kstat_ref[pl.ds(pl.multiple_of(h * 8, 8), 8), :][0:1, 0:1]
        bound2 = jnp.max(qstat_ref[...]) * jnp.max(kstat_ref[...]) * (FIXED_MARGIN * FIXED_MARGIN)
        use_fixed = bound2 <= FIXED_LIMIT * FIXED_LIMIT
        fixed_sc[0] = use_fixed.astype(jnp.int32)
        acc_sc[...] = jnp.zeros(acc_sc.shape, F32)
        l_sc[...] = jnp.zeros(l_sc.shape, F32)
        for g in range(groups):
            qf = q_ref[:, g * dq:(g + 1) * dq].astype(F32)
            qn2 = lax.dot_general(jnp.ones((8, dq), F32), qf * qf, nt, preferred_element_type=F32,
                                  precision=lax.Precision.HIGHEST)[0:1, :]
            m_fix = jnp.sqrt(qn2 * kmax2) * FIXED_MARGIN
            m_sc[g] = jnp.where(use_fixed, m_fix, NEG_INF)

    fixed = fixed_sc[0] == 1
    running = jnp.logical_not(fixed)

    def load_k():
        if n_kparts == 1:
            return k_refs[0][...]
        return jnp.concatenate([r[...] for r in k_refs], axis=-1)

    @pl.when(fixed)
    def _():
        k = load_k()
        vt = vt_ref[...]
        for g in range(groups):
            s_t = lax.dot_general(k, q_ref[:, g * dq:(g + 1) * dq], nt, preferred_element_type=F32)
            p_t = jnp.exp2(s_t - m_sc[g])
            l_sc[g] += jnp.sum(p_t, axis=0, keepdims=True)
            acc_sc[g] += jnp.dot(vt, p_t.astype(BF16), preferred_element_type=F32)

    @pl.when(running)
    def _():
        k = load_k()
        vt = vt_ref[...]
        for g in range(groups):
            s_t = lax.dot_general(k, q_ref[:, g * dq:(g + 1) * dq], nt, preferred_element_type=F32)
            m_prev = m_sc[g]
            m_new = jnp.maximum(m_prev, jnp.max(s_t, axis=0, keepdims=True))
            alpha = jnp.exp2(m_prev - m_new)
            p_t = jnp.exp2(s_t - m_new)
            l_sc[g] = alpha * l_sc[g] + jnp.sum(p_t, axis=0, keepdims=True)
            acc_sc[g] = alpha * acc_sc[g] + jnp.dot(vt, p_t.astype(BF16), preferred_element_type=F32)
            m_sc[g] = m_new

    @pl.when(kv == nkv - 1)
    def _():
        for g in range(groups):
            o_ref[:, g * dv:(g + 1) * dv] = (acc_sc[g] / l_sc[g]).T.astype(o_ref.dtype)


def _flash(q, vt, qstat, kstat, *, n_heads_kv, groups, dq, dv, q_row0, nq, kv_row0, nkv_rows, tq, tk, k_cols):
    nkv = nkv_rows // tk
    qb0, kb0 = q_row0 // tq, kv_row0 // tk
    in_specs = [pl.BlockSpec((tq, groups * dq), lambda h, i, j: (qb0 + i, h))]
    args = [q]
    for (arr, width, per_head, col0) in k_cols:
        if per_head:
            in_specs.append(pl.BlockSpec((tk, width), lambda h, i, j, c=col0: (kb0 + j, c + h)))
        else:
            in_specs.append(pl.BlockSpec((tk, width), lambda h, i, j, c=col0: (kb0 + j, c)))
        args.append(arr)
    in_specs.append(pl.BlockSpec((dv, tk), lambda h, i, j: (h, kb0 + j)))
    in_specs.append(pl.BlockSpec(qstat.shape, lambda h, i, j: (0, 0)))
    in_specs.append(pl.BlockSpec(kstat.shape, lambda h, i, j: (0, 0)))
    args += [vt, qstat, kstat]
    return pl.pallas_call(
        functools.partial(_flash_kernel, groups=groups, dq=dq, dv=dv, nkv=nkv, n_kparts=len(k_cols)),
        grid=(n_heads_kv, nq // tq, nkv),
        in_specs=in_specs,
        out_specs=pl.BlockSpec((tq, groups * dv), lambda h, i, j: (i, h)),
        out_shape=jax.ShapeDtypeStruct((nq, n_heads_kv * groups * dv), BF16),
        scratch_shapes=[pltpu.VMEM((groups, 1, tq), F32),
                        pltpu.VMEM((groups, 1, tq), F32),
                        pltpu.VMEM((groups, dv, tq), F32),
                        pltpu.SMEM((1,), jnp.int32)],
        compiler_params=_params("parallel", "parallel", "arbitrary"),
        name="flash",
    )(*args)


def _attend(q, k_cols, vt, qstat, kstat, *, n_heads_kv, groups, dq, dv, n_lat, n_ctx):
    t = n_lat + n_ctx
    tq = _pick(n_lat, (512, 256)) if groups > 1 else _pick(n_lat, (1024, 512, 256))
    tk = _pick(t, (1280, 640, 256))
    common = dict(n_heads_kv=n_heads_kv, groups=groups, dq=dq, dv=dv, k_cols=k_cols)
    o_x = _flash(q, vt, qstat, kstat, q_row0=0, nq=n_lat, kv_row0=0, nkv_rows=t, tq=tq, tk=tk, **common)
    o_c = _flash(q, vt, qstat, kstat, q_row0=n_lat, nq=n_ctx, kv_row0=n_lat, nkv_rows=n_ctx, tq=n_ctx,
                 tk=n_ctx, **common)
    return jnp.concatenate([o_x, o_c], axis=0)


def _gqa_prep_kernel(qkv_ref, qg_ref, kg_ref, cos_ref, sin_ref, q_ref, k_ref, v_ref, qstat_ref, kstat_ref, *,
                     qscale):
    cos = cos_ref[...]
    sin = sin_ref[...]
    hd = ATT_HEAD_DIM

    @pl.when(pl.program_id(0) == 0)
    def _():
        qstat_ref[...] = jnp.zeros(qstat_ref.shape, F32)
        kstat_ref[...] = jnp.zeros(kstat_ref.shape, F32)

    def head(col, gain):
        x = qkv_ref[:, col:col + hd]
        y = x * lax.rsqrt(jnp.mean(x * x, axis=-1, keepdims=True) + EPS) * gain
        return y * cos + pltpu.roll(y, hd // 2, 1) * sin

    for h in range(ATT_HEADS):
        y = (head(h * hd, qg_ref[...]) * qscale).astype(BF16)
        q_ref[:, h * hd:(h + 1) * hd] = y
        _stat_update(qstat_ref, h, y)
    k0 = ATT_HEADS * hd
    for h in range(ATT_KV_HEADS):
        y = head(k0 + h * hd, kg_ref[...]).astype(BF16)
        k_ref[:, h * hd:(h + 1) * hd] = y
        _stat_update(kstat_ref, h, y)
    v0 = k0 + ATT_KV_HEADS * hd
    v_ref[...] = qkv_ref[:, v0:v0 + ATT_KV_HEADS * hd].astype(BF16)


def _gqa_prep(qkv, q_gain, k_gain, cos, sin):
    t = qkv.shape[0]
    tm = ROW_TILE
    nq, nk = ATT_HEADS * ATT_HEAD_DIM, ATT_KV_HEADS * ATT_HEAD_DIM
    qscale = ATT_HEAD_DIM ** -0.5 * LOG2E
    return pl.pallas_call(
        functools.partial(_gqa_prep_kernel, qscale=qscale),
        grid=(t // tm,),
        in_specs=[pl.BlockSpec((tm, nq + 2 * nk), lambda i: (i, 0)),
                  pl.BlockSpec((1, ATT_HEAD_DIM), lambda i: (0, 0)),
                  pl.BlockSpec((1, ATT_HEAD_DIM), lambda i: (0, 0)),
                  pl.BlockSpec((tm, ATT_HEAD_DIM), lambda i: (i, 0)),
                  pl.BlockSpec((tm, ATT_HEAD_DIM), lambda i: (i, 0))],
        out_specs=[pl.BlockSpec((tm, nq), lambda i: (i, 0)),
                   pl.BlockSpec((tm, nk), lambda i: (i, 0)),
                   pl.BlockSpec((tm, nk), lambda i: (i, 0)),
                   pl.BlockSpec((ATT_HEADS * 8, LANES), lambda i: (0, 0)),
                   pl.BlockSpec((ATT_KV_HEADS * 8, LANES), lambda i: (0, 0))],
        out_shape=[jax.ShapeDtypeStruct((t, nq), BF16),
                   jax.ShapeDtypeStruct((t, nk), BF16),
                   jax.ShapeDtypeStruct((t, nk), BF16),
                   jax.ShapeDtypeStruct((ATT_HEADS * 8, LANES), F32),
                   jax.ShapeDtypeStruct((ATT_KV_HEADS * 8, LANES), F32)],
        compiler_params=_params("arbitrary"),
        name="gqa_prep",
    )(qkv, q_gain.reshape(1, -1), k_gain.reshape(1, -1), cos, sin)


def _gqa_mixer(h, w_in, q_gain, k_gain, cos, sin, n_lat, n_ctx):
    qkv = _mm(h, w_in.astype(BF16), F32, tn=512)
    q, k, v, qstat, kstat = _gqa_prep(qkv, q_gain, k_gain, cos, sin)
    return _attend(q, [(k, ATT_HEAD_DIM, True, 0)], v.T, qstat, kstat, n_heads_kv=ATT_KV_HEADS,
                   groups=ATT_HEADS // ATT_KV_HEADS, dq=ATT_HEAD_DIM, dv=ATT_HEAD_DIM,
                   n_lat=n_lat, n_ctx=n_ctx)


def _swap_halves_64(x):
    lane = lax.broadcasted_iota(jnp.int32, x.shape, 1)
    return jnp.where(lane % MLA_ROPE < MLA_ROPE // 2, pltpu.roll(x, LANES - MLA_ROPE // 2, 1),
                     pltpu.roll(x, MLA_ROPE // 2, 1))


def _mla_prep_kernel(qn_ref, qr_ref, kr_ref, kn_ref, cos_ref, sin_ref, qf_ref, kk_ref, qstat_ref, kstat_ref, *,
                     qscale):
    cos = cos_ref[...]
    sin = sin_ref[...]
    lane = lax.broadcasted_iota(jnp.int32, cos.shape, 1)
    low = lane < MLA_ROPE
    wq = MLA_NOPE + LANES

    @pl.when(pl.program_id(0) == 0)
    def _():
        qstat_ref[...] = jnp.zeros(qstat_ref.shape, F32)
        kstat_ref[...] = jnp.zeros(kstat_ref.shape, F32)

    kr = jnp.where(low, kr_ref[...], 0.0)
    ky = kr * cos + _swap_halves_64(kr) * sin
    kk = (ky + pltpu.roll(ky, MLA_ROPE, 1)).astype(BF16)
    kk_ref[...] = kk
    for j in range(MLA_HEADS // 2):
        x = qr_ref[:, j * LANES:(j + 1) * LANES]
        y = (x * cos + _swap_halves_64(x) * sin) * qscale
        for h, yr in ((2 * j, jnp.where(low, y, 0.0)), (2 * j + 1, jnp.where(low, 0.0, y))):
            qh = jnp.concatenate([qn_ref[:, h * MLA_NOPE:(h + 1) * MLA_NOPE] * qscale, yr], axis=-1).astype(BF16)
            qf_ref[:, h * wq:(h + 1) * wq] = qh
            _stat_update(qstat_ref, h, qh)
            _stat_update(kstat_ref, h, jnp.concatenate([kn_ref[:, h * MLA_NOPE:(h + 1) * MLA_NOPE], kk], axis=-1))


def _mla_prep(qp, t_in, kvp, cos, sin):
    t = qp.shape[0]
    tm = ROW_TILE
    nn, nr = MLA_HEADS * MLA_NOPE, MLA_HEADS * MLA_ROPE
    qscale = (MLA_NOPE + MLA_ROPE) ** -0.5 * LOG2E
    kr_block = (MLA_Q_RANK + MLA_KV_RANK) // LANES
    return pl.pallas_call(
        functools.partial(_mla_prep_kernel, qscale=qscale),
        grid=(t // tm,),
        in_specs=[pl.BlockSpec((tm, nn), lambda i: (i, 0)),
                  pl.BlockSpec((tm, nr), lambda i: (i, nn // nr)),
                  pl.BlockSpec((tm, LANES), lambda i: (i, kr_block)),
                  pl.BlockSpec((tm, nn), lambda i: (i, 0)),
                  pl.BlockSpec((tm, LANES), lambda i: (i, 0)),
                  pl.BlockSpec((tm, LANES), lambda i: (i, 0))],
        out_specs=[pl.BlockSpec((tm, MLA_HEADS * (MLA_NOPE + LANES)), lambda i: (i, 0)),
                   pl.BlockSpec((tm, LANES), lambda i: (i, 0)),
                   pl.BlockSpec((MLA_HEADS * 8, LANES), lambda i: (0, 0)),
                   pl.BlockSpec((MLA_HEADS * 8, LANES), lambda i: (0, 0))],
        out_shape=[jax.ShapeDtypeStruct((t, MLA_HEADS * (MLA_NOPE + LANES)), BF16),
                   jax.ShapeDtypeStruct((t, LANES), BF16),
                   jax.ShapeDtypeStruct((MLA_HEADS * 8, LANES), F32),
                   jax.ShapeDtypeStruct((MLA_HEADS * 8, LANES), F32)],
        compiler_params=_params("arbitrary"),
        name="mla_prep",
    )(qp, qp, t_in, kvp, cos, sin)


def _mla_mixer(h, w_in, q_gain, kv_gain, w_qb, w_kvb, cos, sin, n_lat, n_ctx):
    pad = -w_in.shape[1] % 512
    t_in = _mm(h, jnp.pad(w_in, ((0, 0), (0, pad))).astype(BF16), F32, tn=512)
    cq = _rms_rows(t_in, q_gain, BF16, width=MLA_Q_RANK, col_block=0)
    ckv = _rms_rows(t_in, kv_gain, BF16, width=MLA_KV_RANK, col_block=MLA_Q_RANK // MLA_KV_RANK)
    wq = w_qb.reshape(MLA_Q_RANK, MLA_HEADS, MLA_NOPE + MLA_ROPE)
    wq = jnp.concatenate([wq[:, :, :MLA_NOPE].reshape(MLA_Q_RANK, -1),
                          wq[:, :, MLA_NOPE:].reshape(MLA_Q_RANK, -1)], axis=1).astype(BF16)
    wkv = w_kvb.reshape(MLA_KV_RANK, MLA_HEADS, MLA_NOPE + MLA_V)
    wkv = jnp.concatenate([wkv[:, :, :MLA_NOPE].reshape(MLA_KV_RANK, -1),
                           wkv[:, :, MLA_NOPE:].reshape(MLA_KV_RANK, -1)], axis=1).astype(BF16)
    qp = _mm(cq, wq, F32, tn=512)
    kvp = _mm(ckv, wkv, BF16, tn=512)
    qf, kk, qstat, kstat = _mla_prep(qp, t_in, kvp, cos, sin)
    vt = kvp[:, MLA_HEADS * MLA_NOPE:].T
    return _attend(qf, [(kvp, MLA_NOPE, True, 0), (kk, LANES, False, 0)], vt, qstat, kstat,
                   n_heads_kv=MLA_HEADS, groups=1, dq=MLA_NOPE + LANES, dv=MLA_V,
                   n_lat=n_lat, n_ctx=n_ctx)


def _log_sigmoid(x):
    return jnp.minimum(x, 0.0) - jnp.log1p(jnp.exp(-jnp.abs(x)))


def _mlstm_kernel(q_ref, k_ref, v_ref, gc_ref, gr_ref, h_ref, c_sc, m_sc, *, rev):
    L, dk, dvv = ML_L, ML_QK_DIM, ML_V_DIM
    step = pl.program_id(0)

    @pl.when(step == 0)
    def _():
        c_sc[...] = jnp.zeros(c_sc.shape, F32)
        m_sc[...] = jnp.zeros(m_sc.shape, F32)

    t_idx = lax.broadcasted_iota(jnp.int32, (L, L), 0)
    s_idx = lax.broadcasted_iota(jnp.int32, (L, L), 1)
    mask = (s_idx >= t_idx) if rev else (s_idx <= t_idx)
    tri = mask.astype(F32)
    gc = gc_ref[...]
    gr = gr_ref[...]
    b_col = jnp.dot(tri, _log_sigmoid(gc), preferred_element_type=F32, precision=lax.Precision.HIGHEST)
    b_row = lax.dot_general(_log_sigmoid(gr), tri, (((1,), (1,)), ((), ())),
                            preferred_element_type=F32, precision=lax.Precision.HIGHEST)
    last = 0 if rev else L - 1
    i_base, f_base = (2 * ML_HEADS, 3 * ML_HEADS) if rev else (0, ML_HEADS)
    lane_e = lax.broadcasted_iota(jnp.int32, (L, LANES), 1)
    ones_col = jnp.where(lane_e == 0, 1.0, 0.0).astype(BF16)

    for hh in range(ML_HEADS):
        ci, cf = i_base + hh, f_base + hh
        bc = b_col[:, cf:cf + 1]
        br = b_row[cf:cf + 1, :]
        lic = gc[:, ci:ci + 1]
        lir = gr[ci:ci + 1, :]
        g = bc[last:last + 1, :]
        m_prev = m_sc[hh][0:1, 0:1]
        q = q_ref[:, hh * dk:(hh + 1) * dk].astype(BF16)
        kf = k_ref[:, hh * dk:(hh + 1) * dk] * (dk ** -0.5)
        v_ext = jnp.concatenate([v_ref[:, hh * dvv:(hh + 1) * dvv].astype(BF16), ones_col], axis=-1)

        d_log = jnp.where(mask, bc - br + lir, NEG_INF)
        inter = bc + m_prev
        m_t = jnp.maximum(inter, jnp.max(d_log, axis=-1, keepdims=True))
        w_intra = jnp.exp(d_log - m_t)
        w_inter = jnp.exp(inter - m_t)
        s = lax.dot_general(q, kf.astype(BF16), (((1,), (1,)), ((), ())), preferred_element_type=F32) * w_intra
        c_old = c_sc[hh]
        ext = w_inter * jnp.dot(q, c_old.astype(BF16), preferred_element_type=F32) \
            + jnp.dot(s.astype(BF16), v_ext, preferred_element_type=F32)
        den = ext[:, dvv:dvv + 1]
        h_ref[:, hh * dvv:(hh + 1) * dvv] = ext[:, :dvv] / jnp.maximum(jnp.abs(den), jnp.exp(-m_t))

        a = g - bc + lic
        m_new = jnp.maximum(g + m_prev, jnp.max(a, axis=0, keepdims=True))
        w_s = jnp.exp(a - m_new)
        decay = jnp.exp(g + m_prev - m_new)
        kw = (kf * w_s).astype(BF16)
        c_sc[hh] = decay * c_old + lax.dot_general(kw, v_ext, (((0,), (0,)), ((), ())),
                                                   preferred_element_type=F32)
        m_sc[hh] = jnp.broadcast_to(m_new, m_sc.shape[1:])


def _mlstm_scan(qkvo, gates, gates_t, n_lat, n_ctx, rev):
    t = n_lat + n_ctx
    L = ML_L
    nlat, nctx = n_lat // L, n_ctx // L
    nqk, nv = ML_HEADS * ML_QK_DIM, ML_HEADS * ML_V_DIM

    def chunk(c):
        if rev:
            return nlat + nctx - 1 - c
        return jnp.where(c < nctx, nlat + c, c - nctx)

    return pl.pallas_call(
        functools.partial(_mlstm_kernel, rev=rev),
        grid=(nlat + nctx,),
        in_specs=[pl.BlockSpec((L, nqk), lambda c: (chunk(c), 0)),
                  pl.BlockSpec((L, nqk), lambda c: (chunk(c), 1)),
                  pl.BlockSpec((L, nv), lambda c: (chunk(c), (2 * nqk) // nv)),
                  pl.BlockSpec((L, LANES), lambda c: (chunk(c), 0)),
                  pl.BlockSpec((4 * ML_HEADS, L), lambda c: (0, chunk(c)))],
        out_specs=pl.BlockSpec((L, nv), lambda c: (chunk(c), 0)),
        out_shape=jax.ShapeDtypeStruct((t, nv), F32),
        scratch_shapes=[pltpu.VMEM((ML_HEADS, ML_QK_DIM, ML_EXT), F32),
                        pltpu.VMEM((ML_HEADS, 8, LANES), F32)],
        compiler_params=_params("arbitrary"),
        name="mlstm_rev" if rev else "mlstm_fwd",
    )(qkvo, qkvo, qkvo, gates, gates_t)


def _mlstm_finish_kernel(hf_ref, hb_ref, o_ref, g_ref, y_ref):
    dvv = ML_V_DIM
    for hh in range(ML_HEADS):
        sl = slice(hh * dvv, (hh + 1) * dvv)
        h = hf_ref[:, sl] + hb_ref[:, sl]
        hn = h * lax.rsqrt(jnp.mean(h * h, axis=-1, keepdims=True) + EPS) * g_ref[:, sl]
        y_ref[:, sl] = (hn * jax.nn.sigmoid(o_ref[:, sl])).astype(BF16)


def _mlstm_finish(h_f, h_b, qkvo, h_gain):
    t, nv = h_f.shape
    tm = ROW_TILE
    o_block = (2 * ML_HEADS * ML_QK_DIM + nv) // nv
    return pl.pallas_call(
        _mlstm_finish_kernel,
        grid=(t // tm,),
        in_specs=[pl.BlockSpec((tm, nv), lambda i: (i, 0)),
                  pl.BlockSpec((tm, nv), lambda i: (i, 0)),
                  pl.BlockSpec((tm, nv), lambda i: (i, o_block)),
                  pl.BlockSpec((1, nv), lambda i: (0, 0))],
        out_specs=pl.BlockSpec((tm, nv), lambda i: (i, 0)),
        out_shape=jax.ShapeDtypeStruct((t, nv), BF16),
        compiler_params=_params("parallel"),
        name="mlstm_finish",
    )(h_f, h_b, qkvo, h_gain.reshape(1, nv))


def _mlstm_mixer(h, w_in, gate_b, h_gain, n_lat, n_ctx):
    n_main = 2 * ML_HEADS * ML_QK_DIM + 2 * ML_HEADS * ML_V_DIM
    n_g = 4 * ML_HEADS
    qkvo = _mm(h, w_in[:, :n_main].astype(BF16), F32, tn=512)
    w_g = jnp.pad(w_in[:, n_main:], ((0, 0), (0, LANES - n_g))).astype(BF16)
    b_g = jnp.pad(gate_b.astype(F32), (0, LANES - n_g)).reshape(1, LANES)
    gates = _mm(h, w_g, F32, tn=LANES, bias=b_g)
    gates_t = gates[:, :n_g].T
    h_f = _mlstm_scan(qkvo, gates, gates_t, n_lat, n_ctx, False)
    h_b = _mlstm_scan(qkvo, gates, gates_t, n_lat, n_ctx, True)
    return _mlstm_finish(h_f, h_b, qkvo, h_gain)


def _rope_tables(n_lat, n_ctx, rot_dim):
    rows = n_lat // GRID_W
    row = jnp.repeat(jnp.arange(rows, dtype=F32), GRID_W)
    col = jnp.tile(jnp.arange(GRID_W, dtype=F32), rows)
    n_freq = rot_dim // 4
    inv_freq = ROPE_THETA ** (-jnp.arange(n_freq, dtype=F32) / n_freq)
    ang = jnp.concatenate([row[:, None] * inv_freq, col[:, None] * inv_freq], axis=-1)
    cos, sin = jnp.cos(ang), jnp.sin(ang)
    reps = LANES // rot_dim
    cos_t = jnp.tile(jnp.concatenate([cos, cos], axis=-1), (1, reps))
    sin_t = jnp.tile(jnp.concatenate([-sin, sin], axis=-1), (1, reps))
    cos_t = jnp.concatenate([cos_t, jnp.ones((n_ctx, LANES), F32)], axis=0)
    sin_t = jnp.concatenate([sin_t, jnp.zeros((n_ctx, LANES), F32)], axis=0)
    return cos_t, sin_t


def kernel(x, c, ctx, c_ctx, ada_down, ada_up, ada_b, norm_mix, norm_ffn, ffn_in, ffn_out, att_in, att_qnorm,
           att_knorm, att_out, ml_in, ml_gate_b, ml_hnorm, ml_out, mla_in, mla_qnorm, mla_kvnorm, mla_qb,
           mla_kvb, mla_out, final_norm):
    b, n_lat, d = x.shape
    n_ctx = ctx.shape[1]
    depth = ada_down.shape[0]
    assert b == 1 and d == D_MODEL and n_lat % ROW_TILE == 0 and n_ctx % ML_L == 0 and n_lat % ML_L == 0

    cos_a, sin_a = _rope_tables(n_lat, n_ctx, ATT_HEAD_DIM)
    cos_m, sin_m = _rope_tables(n_lat, n_ctx, MLA_ROPE)

    cond = jnp.concatenate([c.reshape(1, d), c_ctx.reshape(1, d), jnp.zeros((6, d), F32)], axis=0)
    mods = _ada_all(cond, ada_down, ada_up, ada_b).reshape(depth, 8, N_MOD, d)

    xs = jnp.concatenate([x[0], ctx[0]], axis=0)
    t = n_lat + n_ctx
    tm_out = _pick(t, (1280, 1024, 640, 512, 256))
    tm_ffn = _pick(t, (640, 512, 256))

    for i in range(depth):
        kind, j = i % 3, i // 3
        mod = mods[i, :2]
        h = _rms_rows(xs, norm_mix[i], BF16, mod=mod, ks=0, kc=1, n_lat=n_lat)
        if kind == 0:
            o = _gqa_mixer(h, att_in[j], att_qnorm[j], att_knorm[j], cos_a, sin_a, n_lat, n_ctx)
            w_o = att_out[j]
        elif kind == 1:
            o = _mlstm_mixer(h, ml_in[j], ml_gate_b[j], ml_hnorm[j], n_lat, n_ctx)
            w_o = ml_out[j]
        else:
            o = _mla_mixer(h, mla_in[j], mla_qnorm[j], mla_kvnorm[j], mla_qb[j], mla_kvb[j], cos_m, sin_m,
                           n_lat, n_ctx)
            w_o = mla_out[j]
        xs = _mm_res(o, w_o.astype(BF16), xs, mod[:, 2], n_lat, tm=tm_out, tn=512, tk=w_o.shape[0])
        h2 = _rms_rows(xs, norm_ffn[i], BF16, mod=mod, ks=3, kc=4, n_lat=n_lat)
        act = _mm_swiglu(h2, ffn_in[i].astype(BF16), tn=256)
        xs = _mm_res(act, ffn_out[i].astype(BF16), xs, mod[:, 5], n_lat, tm=tm_ffn, tn=1024, tk=D_FF // 2)

    out = _rms_rows(xs, final_norm, F32, rows=n_lat)
    return out.reshape(1, n_lat, d)
```

```python
import functools
import math

import jax
import jax.numpy as jnp
from jax import lax
from jax.experimental import pallas as pl
from jax.experimental.pallas import tpu as pltpu

F32 = jnp.float32
BF16 = jnp.bfloat16

D_MODEL = 4096
GRID_W = 64
EPS = 1e-6
ROPE_THETA = 10000.0
ADA_RANK = 256
N_MOD = 6
D_FF = -(-8 * D_MODEL // (3 * 256)) * 256
ATT_HEAD_DIM = 128
ATT_HEADS = D_MODEL // ATT_HEAD_DIM
ATT_KV_HEADS = ATT_HEADS // 4
ML_HEADS = 8
ML_QK_DIM = D_MODEL // 2 // ML_HEADS
ML_V_DIM = D_MODEL // ML_HEADS
MLA_HEADS = D_MODEL // 128
MLA_Q_RANK = D_MODEL // 4
MLA_KV_RANK = 512
MLA_NOPE = 128
MLA_ROPE = 64
MLA_V = 128

LANES = 128
VMEM_LIMIT = 56 * 1024 * 1024
LOG2E = 1.4426950408889634
NEG_INF = float("-inf")
FIXED_MARGIN = 1.02
FIXED_LIMIT = 60.0
ROW_TILE = 256
ML_L = 256
ML_EXT = ML_V_DIM + LANES


def _params(*sem):
    return pltpu.CompilerParams(dimension_semantics=sem, vmem_limit_bytes=VMEM_LIMIT)


def _pick(n, cands):
    for c in cands:
        if n % c == 0:
            return c
    raise ValueError(f"no tile for {n} in {cands}")


def _ada_kernel(cond_ref, wd_ref, wu_ref, b_ref, o_ref):
    c = cond_ref[...]
    a = (c * jax.nn.sigmoid(c)).astype(BF16)
    t = jnp.dot(a, wd_ref[...].astype(BF16), preferred_element_type=F32)
    o_ref[...] = jnp.dot(t.astype(BF16), wu_ref[...].astype(BF16), preferred_element_type=F32) + b_ref[...]


def _ada_all(cond, ada_down, ada_up, ada_b):
    depth, d, r = ada_down.shape
    n6 = ada_up.shape[-1]
    tn = 2048
    return pl.pallas_call(
        _ada_kernel,
        grid=(depth, n6 // tn),
        in_specs=[
            pl.BlockSpec((8, d), lambda l, j: (0, 0)),
            pl.BlockSpec((None, d, r), lambda l, j: (l, 0, 0)),
            pl.BlockSpec((None, r, tn), lambda l, j: (l, 0, j)),
            pl.BlockSpec((None, 1, tn), lambda l, j: (l, 0, j)),
        ],
        out_specs=pl.BlockSpec((None, 8, tn), lambda l, j: (l, 0, j)),
        out_shape=jax.ShapeDtypeStruct((depth, 8, n6), F32),
        compiler_params=_params("arbitrary", "arbitrary"),
        name="ada_mod",
    )(cond, ada_down, ada_up, ada_b.reshape(depth, 1, n6))


def _norm_kernel(x_ref, g_ref, *rest, ks, kc):
    o_ref = rest[-1]
    x = x_ref[...].astype(F32)
    y = x * lax.rsqrt(jnp.mean(x * x, axis=-1, keepdims=True) + EPS) * g_ref[...]
    if ks is not None:
        mod_ref = rest[0]
        y = y * (1.0 + mod_ref[kc:kc + 1, :]) + mod_ref[ks:ks + 1, :]
    o_ref[...] = y.astype(o_ref.dtype)


def _rms_rows(x, gain, out_dtype, *, width=None, col_block=0, rows=None, mod=None, ks=None, kc=None,
              n_lat=None):
    t = x.shape[0] if rows is None else rows
    width = x.shape[1] if width is None else width
    tm = ROW_TILE
    in_specs = [pl.BlockSpec((tm, width), lambda i: (i, col_block)),
                pl.BlockSpec((1, width), lambda i: (0, 0))]
    args = [x, gain.reshape(1, width).astype(F32)]
    if mod is not None:
        nb_lat = n_lat // tm
        in_specs.append(pl.BlockSpec((None, N_MOD, width), lambda i: (jnp.where(i >= nb_lat, 1, 0), 0, 0)))
        args.append(mod)
    return pl.pallas_call(
        functools.partial(_norm_kernel, ks=ks, kc=kc),
        grid=(t // tm,),
        in_specs=in_specs,
        out_specs=pl.BlockSpec((tm, width), lambda i: (i, 0)),
        out_shape=jax.ShapeDtypeStruct((t, width), out_dtype),
        compiler_params=_params("parallel"),
        name="rms_rows",
    )(*args)


def _mm_kernel(a_ref, w_ref, *rest, has_bias):
    o_ref = rest[-1]
    acc = jnp.dot(a_ref[...], w_ref[...].astype(BF16), preferred_element_type=F32)
    if has_bias:
        acc = acc + rest[0][...]
    o_ref[...] = acc.astype(o_ref.dtype)


def _wspec(w, layer, block, index):
    if layer is None:
        return pl.BlockSpec(block, index)
    return pl.BlockSpec((None,) + block, lambda *g: (layer,) + index(*g))


def _mm(a, w, out_dtype, *, tn, bias=None, n_out=None, layer=None):
    m, k = a.shape
    n = w.shape[-1] if n_out is None else n_out
    tm = _pick(m, (1280, 1024, 640, 512, 256))
    in_specs = [pl.BlockSpec((tm, k), lambda i, j: (i, 0)),
                _wspec(w, layer, (k, tn), lambda i, j: (0, j))]
    args = [a, w]
    if bias is not None:
        in_specs.append(pl.BlockSpec((1, tn), lambda i, j: (0, j)))
        args.append(bias)
    return pl.pallas_call(
        functools.partial(_mm_kernel, has_bias=bias is not None),
        grid=(m // tm, n // tn),
        in_specs=in_specs,
        out_specs=pl.BlockSpec((tm, tn), lambda i, j: (i, j)),
        out_shape=jax.ShapeDtypeStruct((m, n), out_dtype),
        compiler_params=_params("parallel", "arbitrary"),
        name="mm",
    )(*args)


def _mm_res_kernel(a_ref, w_ref, res_ref, gate_ref, o_ref, *scratch, nk, tm, n_lat):
    part = jnp.dot(a_ref[...], w_ref[...].astype(BF16), preferred_element_type=F32)

    def finish(acc):
        row = pl.program_id(0) * tm + lax.broadcasted_iota(jnp.int32, (tm, 1), 0)
        gate = jnp.where(row >= n_lat, gate_ref[1:2, :], gate_ref[0:1, :])
        o_ref[...] = res_ref[...] + gate * acc

    if nk == 1:
        finish(part)
    else:
        acc_ref = scratch[0]
        kk = pl.program_id(2)

        @pl.when(kk == 0)
        def _():
            acc_ref[...] = part

        @pl.when(jnp.logical_and(kk > 0, kk < nk - 1))
        def _():
            acc_ref[...] += part

        @pl.when(kk == nk - 1)
        def _():
            finish(acc_ref[...] + part)


def _mm_res(a, w, res, gate, n_lat, *, tm, tn, tk, layer=None):
    m, k = a.shape
    n = w.shape[-1]
    nk = k // tk
    scratch = [pltpu.VMEM((tm, tn), F32)] if nk > 1 else []
    return pl.pallas_call(
        functools.partial(_mm_res_kernel, nk=nk, tm=tm, n_lat=n_lat),
        grid=(m // tm, n // tn, nk),
        in_specs=[pl.BlockSpec((tm, tk), lambda i, j, kk: (i, kk)),
                  _wspec(w, layer, (tk, tn), lambda i, j, kk: (kk, j)),
                  pl.BlockSpec((tm, tn), lambda i, j, kk: (i, j)),
                  pl.BlockSpec((2, tn), lambda i, j, kk: (0, j))],
        out_specs=pl.BlockSpec((tm, tn), lambda i, j, kk: (i, j)),
        out_shape=jax.ShapeDtypeStruct((m, n), F32),
        scratch_shapes=scratch,
        compiler_params=_params("parallel", "arbitrary", "arbitrary"),
        name="mm_res",
    )(a, w, res, gate)


def _mm_swiglu_kernel(a_ref, wa_ref, wb_ref, o_ref):
    a = a_ref[...]
    u = jnp.dot(a, wa_ref[...].astype(BF16), preferred_element_type=F32)
    v = jnp.dot(a, wb_ref[...].astype(BF16), preferred_element_type=F32)
    o_ref[...] = (u * jax.nn.sigmoid(u) * v).astype(o_ref.dtype)


def _mm_swiglu(a, w, *, tn, layer=None):
    m, k = a.shape
    f = w.shape[-1] // 2
    nb = f // tn
    tm = _pick(m, (1280, 1024, 640, 512, 256))
    return pl.pallas_call(
        _mm_swiglu_kernel,
        grid=(m // tm, nb),
        in_specs=[pl.BlockSpec((tm, k), lambda i, j: (i, 0)),
                  _wspec(w, layer, (k, tn), lambda i, j: (0, j)),
                  _wspec(w, layer, (k, tn), lambda i, j: (0, j + nb))],
        out_specs=pl.BlockSpec((tm, tn), lambda i, j: (i, j)),
        out_shape=jax.ShapeDtypeStruct((m, f), BF16),
        compiler_params=_params("parallel", "arbitrary"),
        name="mm_swiglu",
    )(a, w, w)


def _stat_update(stat_ref, head, y_bf16):
    yf = y_bf16.astype(F32)
    n2 = jnp.max(jnp.sum(yf * yf, axis=-1, keepdims=True), axis=0, keepdims=True)
    sl = slice(head * 8, (head + 1) * 8)
    stat_ref[sl, :] = jnp.maximum(stat_ref[sl, :], jnp.broadcast_to(n2, (8, LANES)))


def _flash_kernel(q_ref, *rest, groups, dq, dv, nkv, n_kparts, has_into):
    k_refs = rest[:n_kparts]
    vt_ref, qstat_ref, kstat_ref = rest[n_kparts:n_kparts + 3]
    o_ref, m_sc, l_sc, acc_sc, fixed_sc = rest[n_kparts + 3 + int(has_into):]
    h = pl.program_id(0)
    kv = pl.program_id(2)
    nt = (((1,), (1,)), ((), ()))

    @pl.when(kv == 0)
    def _():
        kmax2 = kstat_ref[pl.ds(pl.multiple_of(h * 8, 8), 8), :][0:1, 0:1]
        bound2 = jnp.max(qstat_ref[...]) * jnp.max(kstat_ref[...]) * (FIXED_MARGIN * FIXED_MARGIN)
        use_fixed = bound2 <= FIXED_LIMIT * FIXED_LIMIT
        fixed_sc[0] = use_fixed.astype(jnp.int32)
        acc_sc[...] = jnp.zeros(acc_sc.shape, F32)
        l_sc[...] = jnp.zeros(l_sc.shape, F32)
        for g in range(groups):
            qf = q_ref[:, g * dq:(g + 1) * dq].astype(F32)
            qn2 = lax.dot_general(jnp.ones((8, dq), F32), qf * qf, nt, preferred_element_type=F32,
                                  precision=lax.Precision.HIGHEST)[0:1, :]
            m_fix = jnp.sqrt(qn2 * kmax2) * FIXED_MARGIN
            m_sc[g] = jnp.where(use_fixed, m_fix, NEG_INF)

    fixed = fixed_sc[0] == 1
    running = jnp.logical_not(fixed)

    def load_k():
        if n_kparts == 1:
            return k_refs[0][...]
        return jnp.concatenate([r[...] for r in k_refs], axis=-1)

    @pl.when(fixed)
    def _():
        k = load_k()
        vt = vt_ref[...]
        for g in range(groups):
            s_t = lax.dot_general(k, q_ref[:, g * dq:(g + 1) * dq], nt, preferred_element_type=F32)
            p_t = jnp.exp2(s_t - m_sc[g])
            l_sc[g] += jnp.sum(p_t, axis=0, keepdims=True)
            acc_sc[g] += jnp.dot(vt, p_t.astype(BF16), preferred_element_type=F32)

    @pl.when(running)
    def _():
        k = load_k()
        vt = vt_ref[...]
        for g in range(groups):
            s_t = lax.dot_general(k, q_ref[:, g * dq:(g + 1) * dq], nt, preferred_element_type=F32)
            m_prev = m_sc[g]
            m_new = jnp.maximum(m_prev, jnp.max(s_t, axis=0, keepdims=True))
            alpha = jnp.exp2(m_prev - m_new)
            p_t = jnp.exp2(s_t - m_new)
            l_sc[g] = alpha * l_sc[g] + jnp.sum(p_t, axis=0, keepdims=True)
            acc_sc[g] = alpha * acc_sc[g] + jnp.dot(vt, p_t.astype(BF16), preferred_element_type=F32)
            m_sc[g] = m_new

    @pl.when(kv == nkv - 1)
    def _():
        for g in range(groups):
            o_ref[:, g * dv:(g + 1) * dv] = (acc_sc[g] / l_sc[g]).T.astype(o_ref.dtype)


def _flash(q, vt, qstat, kstat, *, n_heads_kv, groups, dq, dv, q_row0, nq, kv_row0, nkv_rows, tq, tk, k_cols,
           into=None):
    nkv = nkv_rows // tk
    qb0, kb0 = q_row0 // tq, kv_row0 // tk
    in_specs = [pl.BlockSpec((tq, groups * dq), lambda h, i, j: (qb0 + i, h))]
    args = [q]
    for (arr, width, per_head, col0) in k_cols:
        if per_head:
            in_specs.append(pl.BlockSpec((tk, width), lambda h, i, j, c=col0: (kb0 + j, c + h)))
        else:
            in_specs.append(pl.BlockSpec((tk, width), lambda h, i, j, c=col0: (kb0 + j, c)))
        args.append(arr)
    in_specs.append(pl.BlockSpec((dv, tk), lambda h, i, j: (h, kb0 + j)))
    in_specs.append(pl.BlockSpec(qstat.shape, lambda h, i, j: (0, 0)))
    in_specs.append(pl.BlockSpec(kstat.shape, lambda h, i, j: (0, 0)))
    args += [vt, qstat, kstat]
    aliases = {}
    if into is not None:
        in_specs.append(pl.BlockSpec(memory_space=pl.ANY))
        aliases = {len(args): 0}
        args.append(into)
    return pl.pallas_call(
        functools.partial(_flash_kernel, groups=groups, dq=dq, dv=dv, nkv=nkv, n_kparts=len(k_cols),
                          has_into=into is not None),
        grid=(n_heads_kv, nq // tq, nkv),
        in_specs=in_specs,
        out_specs=pl.BlockSpec((tq, groups * dv), lambda h, i, j: (qb0 + i, h)),
        out_shape=jax.ShapeDtypeStruct((q.shape[0], n_heads_kv * groups * dv), BF16),
        scratch_shapes=[pltpu.VMEM((groups, 1, tq), F32),
                        pltpu.VMEM((groups, 1, tq), F32),
                        pltpu.VMEM((groups, dv, tq), F32),
                        pltpu.SMEM((1,), jnp.int32)],
        input_output_aliases=aliases,
        compiler_params=_params("parallel", "parallel", "arbitrary"),
        name="flash",
    )(*args)


def _attend(q, k_cols, vt, qstat, kstat, *, n_heads_kv, groups, dq, dv, n_lat, n_ctx):
    t = n_lat + n_ctx
    tq = _pick(n_lat, (1024, 512, 256))
    tk = _pick(t, (1280, 640, 256)) if groups > 1 else _pick(t, (3328, 1280, 640, 256))
    common = dict(n_heads_kv=n_heads_kv, groups=groups, dq=dq, dv=dv, k_cols=k_cols)
    o = _flash(q, vt, qstat, kstat, q_row0=0, nq=n_lat, kv_row0=0, nkv_rows=t, tq=tq, tk=tk, **common)
    return _flash(q, vt, qstat, kstat, q_row0=n_lat, nq=n_ctx, kv_row0=n_lat, nkv_rows=n_ctx, tq=n_ctx,
                  tk=n_ctx, into=o, **common)


def _gqa_prep_kernel(qkv_ref, qg_ref, kg_ref, cos_ref, sin_ref, q_ref, k_ref, v_ref, qstat_ref, kstat_ref, *,
                     qscale):
    cos = cos_ref[...]
    sin = sin_ref[...]
    hd = ATT_HEAD_DIM

    @pl.when(pl.program_id(0) == 0)
    def _():
        qstat_ref[...] = jnp.zeros(qstat_ref.shape, F32)
        kstat_ref[...] = jnp.zeros(kstat_ref.shape, F32)

    def head(col, gain):
        x = qkv_ref[:, col:col + hd]
        y = x * lax.rsqrt(jnp.mean(x * x, axis=-1, keepdims=True) + EPS) * gain
        return y * cos + pltpu.roll(y, hd // 2, 1) * sin

    for h in range(ATT_HEADS):
        y = (head(h * hd, qg_ref[...]) * qscale).astype(BF16)
        q_ref[:, h * hd:(h + 1) * hd] = y
        _stat_update(qstat_ref, h, y)
    k0 = ATT_HEADS * hd
    for h in range(ATT_KV_HEADS):
        y = head(k0 + h * hd, kg_ref[...]).astype(BF16)
        k_ref[:, h * hd:(h + 1) * hd] = y
        _stat_update(kstat_ref, h, y)
    v0 = k0 + ATT_KV_HEADS * hd
    v_ref[...] = qkv_ref[:, v0:v0 + ATT_KV_HEADS * hd].astype(BF16)


def _gqa_prep(qkv, q_gain, k_gain, cos, sin):
    t = qkv.shape[0]
    tm = ROW_TILE
    nq, nk = ATT_HEADS * ATT_HEAD_DIM, ATT_KV_HEADS * ATT_HEAD_DIM
    qscale = ATT_HEAD_DIM ** -0.5 * LOG2E
    return pl.pallas_call(
        functools.partial(_gqa_prep_kernel, qscale=qscale),
        grid=(t // tm,),
        in_specs=[pl.BlockSpec((tm, nq + 2 * nk), lambda i: (i, 0)),
                  pl.BlockSpec((1, ATT_HEAD_DIM), lambda i: (0, 0)),
                  pl.BlockSpec((1, ATT_HEAD_DIM), lambda i: (0, 0)),
                  pl.BlockSpec((tm, ATT_HEAD_DIM), lambda i: (i, 0)),
                  pl.BlockSpec((tm, ATT_HEAD_DIM), lambda i: (i, 0))],
        out_specs=[pl.BlockSpec((tm, nq), lambda i: (i, 0)),
                   pl.BlockSpec((tm, nk), lambda i: (i, 0)),
                   pl.BlockSpec((tm, nk), lambda i: (i, 0)),
                   pl.BlockSpec((ATT_HEADS * 8, LANES), lambda i: (0, 0)),
                   pl.BlockSpec((ATT_KV_HEADS * 8, LANES), lambda i: (0, 0))],
        out_shape=[jax.ShapeDtypeStruct((t, nq), BF16),
                   jax.ShapeDtypeStruct((t, nk), BF16),
                   jax.ShapeDtypeStruct((t, nk), BF16),
                   jax.ShapeDtypeStruct((ATT_HEADS * 8, LANES), F32),
                   jax.ShapeDtypeStruct((ATT_KV_HEADS * 8, LANES), F32)],
        compiler_params=_params("arbitrary"),
        name="gqa_prep",
    )(qkv, q_gain.reshape(1, -1), k_gain.reshape(1, -1), cos, sin)


def _gqa_mixer(h, w_in_all, layer, q_gain, k_gain, cos, sin, n_lat, n_ctx):
    qkv = _mm(h, w_in_all, F32, tn=512, layer=layer)
    q, k, v, qstat, kstat = _gqa_prep(qkv, q_gain, k_gain, cos, sin)
    return _attend(q, [(k, ATT_HEAD_DIM, True, 0)], v.T, qstat, kstat, n_heads_kv=ATT_KV_HEADS,
                   groups=ATT_HEADS // ATT_KV_HEADS, dq=ATT_HEAD_DIM, dv=ATT_HEAD_DIM,
                   n_lat=n_lat, n_ctx=n_ctx)


def _swap_halves_64(x):
    lane = lax.broadcasted_iota(jnp.int32, x.shape, 1)
    return jnp.where(lane % MLA_ROPE < MLA_ROPE // 2, pltpu.roll(x, LANES - MLA_ROPE // 2, 1),
                     pltpu.roll(x, MLA_ROPE // 2, 1))


def _mla_prep_kernel(qn_ref, qr_ref, kr_ref, kn_ref, cos_ref, sin_ref, qf_ref, kk_ref, qstat_ref, kstat_ref, *,
                     qscale):
    cos = cos_ref[...]
    sin = sin_ref[...]
    lane = lax.broadcasted_iota(jnp.int32, cos.shape, 1)
    low = lane < MLA_ROPE
    wq = MLA_NOPE + LANES

    @pl.when(pl.program_id(0) == 0)
    def _():
        qstat_ref[...] = jnp.zeros(qstat_ref.shape, F32)
        kstat_ref[...] = jnp.zeros(kstat_ref.shape, F32)

    kr = jnp.where(low, kr_ref[...], 0.0)
    ky = kr * cos + _swap_halves_64(kr) * sin
    kk = (ky + pltpu.roll(ky, MLA_ROPE, 1)).astype(BF16)
    kk_ref[...] = kk
    for j in range(MLA_HEADS // 2):
        x = qr_ref[:, j * LANES:(j + 1) * LANES]
        y = (x * cos + _swap_halves_64(x) * sin) * qscale
        for h, yr in ((2 * j, jnp.where(low, y, 0.0)), (2 * j + 1, jnp.where(low, 0.0, y))):
            qh = jnp.concatenate([qn_ref[:, h * MLA_NOPE:(h + 1) * MLA_NOPE] * qscale, yr], axis=-1).astype(BF16)
            qf_ref[:, h * wq:(h + 1) * wq] = qh
            _stat_update(qstat_ref, h, qh)
            _stat_update(kstat_ref, h, jnp.concatenate([kn_ref[:, h * MLA_NOPE:(h + 1) * MLA_NOPE], kk], axis=-1))


def _mla_prep(qp, t_in, kvp, cos, sin):
    t = qp.shape[0]
    tm = ROW_TILE
    nn, nr = MLA_HEADS * MLA_NOPE, MLA_HEADS * MLA_ROPE
    qscale = (MLA_NOPE + MLA_ROPE) ** -0.5 * LOG2E
    kr_block = (MLA_Q_RANK + MLA_KV_RANK) // LANES
    return pl.pallas_call(
        functools.partial(_mla_prep_kernel, qscale=qscale),
        grid=(t // tm,),
        in_specs=[pl.BlockSpec((tm, nn), lambda i: (i, 0)),
                  pl.BlockSpec((tm, nr), lambda i: (i, nn // nr)),
                  pl.BlockSpec((tm, LANES), lambda i: (i, kr_block)),
                  pl.BlockSpec((tm, nn), lambda i: (i, 0)),
                  pl.BlockSpec((tm, LANES), lambda i: (i, 0)),
                  pl.BlockSpec((tm, LANES), lambda i: (i, 0))],
        out_specs=[pl.BlockSpec((tm, MLA_HEADS * (MLA_NOPE + LANES)), lambda i: (i, 0)),
                   pl.BlockSpec((tm, LANES), lambda i: (i, 0)),
                   pl.BlockSpec((MLA_HEADS * 8, LANES), lambda i: (0, 0)),
                   pl.BlockSpec((MLA_HEADS * 8, LANES), lambda i: (0, 0))],
        out_shape=[jax.ShapeDtypeStruct((t, MLA_HEADS * (MLA_NOPE + LANES)), BF16),
                   jax.ShapeDtypeStruct((t, LANES), BF16),
                   jax.ShapeDtypeStruct((MLA_HEADS * 8, LANES), F32),
                   jax.ShapeDtypeStruct((MLA_HEADS * 8, LANES), F32)],
        compiler_params=_params("arbitrary"),
        name="mla_prep",
    )(qp, qp, t_in, kvp, cos, sin)


def _mla_mixer(h, w_in, q_gain, kv_gain, w_qb, w_kvb, cos, sin, n_lat, n_ctx):
    pad = -w_in.shape[1] % 512
    t_in = _mm(h, jnp.pad(w_in, ((0, 0), (0, pad))).astype(BF16), F32, tn=512)
    cq = _rms_rows(t_in, q_gain, BF16, width=MLA_Q_RANK, col_block=0)
    ckv = _rms_rows(t_in, kv_gain, BF16, width=MLA_KV_RANK, col_block=MLA_Q_RANK // MLA_KV_RANK)
    wq = w_qb.reshape(MLA_Q_RANK, MLA_HEADS, MLA_NOPE + MLA_ROPE)
    wq = jnp.concatenate([wq[:, :, :MLA_NOPE].reshape(MLA_Q_RANK, -1),
                          wq[:, :, MLA_NOPE:].reshape(MLA_Q_RANK, -1)], axis=1).astype(BF16)
    wkv = w_kvb.reshape(MLA_KV_RANK, MLA_HEADS, MLA_NOPE + MLA_V)
    wkv = jnp.concatenate([wkv[:, :, :MLA_NOPE].reshape(MLA_KV_RANK, -1),
                           wkv[:, :, MLA_NOPE:].reshape(MLA_KV_RANK, -1)], axis=1).astype(BF16)
    qp = _mm(cq, wq, F32, tn=512)
    kvp = _mm(ckv, wkv, BF16, tn=512)
    qf, kk, qstat, kstat = _mla_prep(qp, t_in, kvp, cos, sin)
    vt = kvp[:, MLA_HEADS * MLA_NOPE:].T
    return _attend(qf, [(kvp, MLA_NOPE, True, 0), (kk, LANES, False, 0)], vt, qstat, kstat,
                   n_heads_kv=MLA_HEADS, groups=1, dq=MLA_NOPE + LANES, dv=MLA_V,
                   n_lat=n_lat, n_ctx=n_ctx)


def _log_sigmoid(x):
    return jnp.minimum(x, 0.0) - jnp.log1p(jnp.exp(-jnp.abs(x)))


def _mlstm_kernel(q_ref, k_ref, v_ref, gc_ref, gr_ref, h_ref, c_sc, m_sc, *, rev):
    L, dk, dvv = ML_L, ML_QK_DIM, ML_V_DIM
    step = pl.program_id(0)

    @pl.when(step == 0)
    def _():
        c_sc[...] = jnp.zeros(c_sc.shape, F32)
        m_sc[...] = jnp.zeros(m_sc.shape, F32)

    t_idx = lax.broadcasted_iota(jnp.int32, (L, L), 0)
    s_idx = lax.broadcasted_iota(jnp.int32, (L, L), 1)
    mask = (s_idx >= t_idx) if rev else (s_idx <= t_idx)
    tri = mask.astype(F32)
    gc = gc_ref[...]
    gr = gr_ref[...]
    b_col = jnp.dot(tri, _log_sigmoid(gc), preferred_element_type=F32, precision=lax.Precision.HIGHEST)
    b_row = lax.dot_general(_log_sigmoid(gr), tri, (((1,), (1,)), ((), ())),
                            preferred_element_type=F32, precision=lax.Precision.HIGHEST)
    last = 0 if rev else L - 1
    i_base, f_base = (2 * ML_HEADS, 3 * ML_HEADS) if rev else (0, ML_HEADS)
    lane_e = lax.broadcasted_iota(jnp.int32, (L, LANES), 1)
    ones_col = jnp.where(lane_e == 0, 1.0, 0.0).astype(BF16)

    for hh in range(ML_HEADS):
        ci, cf = i_base + hh, f_base + hh
        bc = b_col[:, cf:cf + 1]
        br = b_row[cf:cf + 1, :]
        lic = gc[:, ci:ci + 1]
        lir = gr[ci:ci + 1, :]
        g = bc[last:last + 1, :]
        m_prev = m_sc[hh][0:1, 0:1]
        q = q_ref[:, hh * dk:(hh + 1) * dk].astype(BF16)
        kf = k_ref[:, hh * dk:(hh + 1) * dk] * (dk ** -0.5)
        v_ext = jnp.concatenate([v_ref[:, hh * dvv:(hh + 1) * dvv].astype(BF16), ones_col], axis=-1)

        d_log = jnp.where(mask, bc - br + lir, NEG_INF)
        inter = bc + m_prev
        m_t = jnp.maximum(inter, jnp.max(d_log, axis=-1, keepdims=True))
        w_intra = jnp.exp(d_log - m_t)
        w_inter = jnp.exp(inter - m_t)
        s = lax.dot_general(q, kf.astype(BF16), (((1,), (1,)), ((), ())), preferred_element_type=F32) * w_intra
        c_old = c_sc[hh]
        ext = w_inter * jnp.dot(q, c_old.astype(BF16), preferred_element_type=F32) \
            + jnp.dot(s.astype(BF16), v_ext, preferred_element_type=F32)
        den = ext[:, dvv:dvv + 1]
        h_ref[:, hh * dvv:(hh + 1) * dvv] = ext[:, :dvv] / jnp.maximum(jnp.abs(den), jnp.exp(-m_t))

        a = g - bc + lic
        m_new = jnp.maximum(g + m_prev, jnp.max(a, axis=0, keepdims=True))
        w_s = jnp.exp(a - m_new)
        decay = jnp.exp(g + m_prev - m_new)
        kw = (kf * w_s).astype(BF16)
        c_sc[hh] = decay * c_old + lax.dot_general(kw, v_ext, (((0,), (0,)), ((), ())),
                                                   preferred_element_type=F32)
        m_sc[hh] = jnp.broadcast_to(m_new, m_sc.shape[1:])


def _mlstm_scan(qkvo, gates, gates_t, n_lat, n_ctx, rev):
    t = n_lat + n_ctx
    L = ML_L
    nlat, nctx = n_lat // L, n_ctx // L
    nqk, nv = ML_HEADS * ML_QK_DIM, ML_HEADS * ML_V_DIM

    def chunk(c):
        if rev:
            return nlat + nctx - 1 - c
        return jnp.where(c < nctx, nlat + c, c - nctx)

    return pl.pallas_call(
        functools.partial(_mlstm_kernel, rev=rev),
        grid=(nlat + nctx,),
        in_specs=[pl.BlockSpec((L, nqk), lambda c: (chunk(c), 0)),
                  pl.BlockSpec((L, nqk), lambda c: (chunk(c), 1)),
                  pl.BlockSpec((L, nv), lambda c: (chunk(c), (2 * nqk) // nv)),
                  pl.BlockSpec((L, LANES), lambda c: (chunk(c), 0)),
                  pl.BlockSpec((4 * ML_HEADS, L), lambda c: (0, chunk(c)))],
        out_specs=pl.BlockSpec((L, nv), lambda c: (chunk(c), 0)),
        out_shape=jax.ShapeDtypeStruct((t, nv), F32),
        scratch_shapes=[pltpu.VMEM((ML_HEADS, ML_QK_DIM, ML_EXT), F32),
                        pltpu.VMEM((ML_HEADS, 8, LANES), F32)],
        compiler_params=_params("arbitrary"),
        name="mlstm_rev" if rev else "mlstm_fwd",
    )(qkvo, qkvo, qkvo, gates, gates_t)


def _mlstm_finish_kernel(hf_ref, hb_ref, o_ref, g_ref, y_ref):
    dvv = ML_V_DIM
    for hh in range(ML_HEADS):
        sl = slice(hh * dvv, (hh + 1) * dvv)
        h = hf_ref[:, sl] + hb_ref[:, sl]
        hn = h * lax.rsqrt(jnp.mean(h * h, axis=-1, keepdims=True) + EPS) * g_ref[:, sl]
        y_ref[:, sl] = (hn * jax.nn.sigmoid(o_ref[:, sl])).astype(BF16)


def _mlstm_finish(h_f, h_b, qkvo, h_gain):
    t, nv = h_f.shape
    tm = ROW_TILE
    o_block = (2 * ML_HEADS * ML_QK_DIM + nv) // nv
    return pl.pallas_call(
        _mlstm_finish_kernel,
        grid=(t // tm,),
        in_specs=[pl.BlockSpec((tm, nv), lambda i: (i, 0)),
                  pl.BlockSpec((tm, nv), lambda i: (i, 0)),
                  pl.BlockSpec((tm, nv), lambda i: (i, o_block)),
                  pl.BlockSpec((1, nv), lambda i: (0, 0))],
        out_specs=pl.BlockSpec((tm, nv), lambda i: (i, 0)),
        out_shape=jax.ShapeDtypeStruct((t, nv), BF16),
        compiler_params=_params("parallel"),
        name="mlstm_finish",
    )(h_f, h_b, qkvo, h_gain.reshape(1, nv))


def _mlstm_mixer(h, w_in_all, layer, gate_b, h_gain, n_lat, n_ctx):
    n_main = 2 * ML_HEADS * ML_QK_DIM + 2 * ML_HEADS * ML_V_DIM
    n_g = 4 * ML_HEADS
    qkvo = _mm(h, w_in_all, F32, tn=512, n_out=n_main, layer=layer)
    w_g = jnp.pad(w_in_all[layer, :, n_main:], ((0, 0), (0, LANES - n_g))).astype(BF16)
    b_g = jnp.pad(gate_b.astype(F32), (0, LANES - n_g)).reshape(1, LANES)
    gates = _mm(h, w_g, F32, tn=LANES, bias=b_g)
    gates_t = gates[:, :n_g].T
    h_f = _mlstm_scan(qkvo, gates, gates_t, n_lat, n_ctx, False)
    h_b = _mlstm_scan(qkvo, gates, gates_t, n_lat, n_ctx, True)
    return _mlstm_finish(h_f, h_b, qkvo, h_gain)


def _rope_tables(n_lat, n_ctx, rot_dim):
    rows = n_lat // GRID_W
    row = jnp.repeat(jnp.arange(rows, dtype=F32), GRID_W)
    col = jnp.tile(jnp.arange(GRID_W, dtype=F32), rows)
    n_freq = rot_dim // 4
    inv_freq = ROPE_THETA ** (-jnp.arange(n_freq, dtype=F32) / n_freq)
    ang = jnp.concatenate([row[:, None] * inv_freq, col[:, None] * inv_freq], axis=-1)
    cos, sin = jnp.cos(ang), jnp.sin(ang)
    reps = LANES // rot_dim
    cos_t = jnp.tile(jnp.concatenate([cos, cos], axis=-1), (1, reps))
    sin_t = jnp.tile(jnp.concatenate([-sin, sin], axis=-1), (1, reps))
    cos_t = jnp.concatenate([cos_t, jnp.ones((n_ctx, LANES), F32)], axis=0)
    sin_t = jnp.concatenate([sin_t, jnp.zeros((n_ctx, LANES), F32)], axis=0)
    return cos_t, sin_t


def kernel(x, c, ctx, c_ctx, ada_down, ada_up, ada_b, norm_mix, norm_ffn, ffn_in, ffn_out, att_in, att_qnorm,
           att_knorm, att_out, ml_in, ml_gate_b, ml_hnorm, ml_out, mla_in, mla_qnorm, mla_kvnorm, mla_qb,
           mla_kvb, mla_out, final_norm):
    b, n_lat, d = x.shape
    n_ctx = ctx.shape[1]
    depth = ada_down.shape[0]
    assert b == 1 and d == D_MODEL and n_lat % ROW_TILE == 0 and n_ctx % ML_L == 0 and n_lat % ML_L == 0

    cos_a, sin_a = _rope_tables(n_lat, n_ctx, ATT_HEAD_DIM)
    cos_m, sin_m = _rope_tables(n_lat, n_ctx, MLA_ROPE)

    cond = jnp.concatenate([c.reshape(1, d), c_ctx.reshape(1, d), jnp.zeros((6, d), F32)], axis=0)
    mods = _ada_all(cond, ada_down, ada_up, ada_b).reshape(depth, 8, N_MOD, d)

    xs = jnp.concatenate([x[0], ctx[0]], axis=0)
    t = n_lat + n_ctx
    tm_out = _pick(t, (1280, 1024, 640, 512, 256))
    tm_ffn = _pick(t, (640, 512, 256))
    ffn_out_bf16 = ffn_out.astype(BF16)

    for i in range(depth):
        kind, j = i % 3, i // 3
        mod = mods[i, :2]
        h = _rms_rows(xs, norm_mix[i], BF16, mod=mod, ks=0, kc=1, n_lat=n_lat)
        if kind == 0:
            o = _gqa_mixer(h, att_in, j, att_qnorm[j], att_knorm[j], cos_a, sin_a, n_lat, n_ctx)
            w_o = att_out
        elif kind == 1:
            o = _mlstm_mixer(h, ml_in, j, ml_gate_b[j], ml_hnorm[j], n_lat, n_ctx)
            w_o = ml_out
        else:
            o = _mla_mixer(h, mla_in[j], mla_qnorm[j], mla_kvnorm[j], mla_qb[j], mla_kvb[j], cos_m, sin_m,
                           n_lat, n_ctx)
            w_o = mla_out
        xs = _mm_res(o, w_o, xs, mod[:, 2], n_lat, tm=tm_out, tn=256, tk=w_o.shape[1], layer=j)
        h2 = _rms_rows(xs, norm_ffn[i], BF16, mod=mod, ks=3, kc=4, n_lat=n_lat)
        act = _mm_swiglu(h2, ffn_in, tn=256, layer=i)
        xs = _mm_res(act, ffn_out_bf16, xs, mod[:, 5], n_lat, tm=tm_ffn, tn=1024, tk=D_FF // 2, layer=i)

    out = _rms_rows(xs, final_norm, F32, rows=n_lat)
    return out.reshape(1, n_lat, d)
```

```python
import functools

import jax
import jax.numpy as jnp
from jax import lax
from jax.experimental import pallas as pl
from jax.experimental.pallas import tpu as pltpu

F32 = jnp.float32
BF16 = jnp.bfloat16

D_MODEL = 4096
GRID_W = 64
EPS = 1e-6
ROPE_THETA = 10000.0
ADA_RANK = 256
N_MOD = 6
D_FF = -(-8 * D_MODEL // (3 * 256)) * 256
ATT_HEAD_DIM = 128
ATT_HEADS = D_MODEL // ATT_HEAD_DIM
ATT_KV_HEADS = ATT_HEADS // 4
ML_HEADS = 8
ML_QK_DIM = D_MODEL // 2 // ML_HEADS
ML_V_DIM = D_MODEL // ML_HEADS
MLA_HEADS = D_MODEL // 128
MLA_Q_RANK = D_MODEL // 4
MLA_KV_RANK = 512
MLA_NOPE = 128
MLA_ROPE = 64
MLA_V = 128

LANES = 128
VMEM_LIMIT = 56 * 1024 * 1024
LOG2E = 1.4426950408889634
NEG_INF = float("-inf")
FIXED_MARGIN = 1.02
FIXED_LIMIT = 60.0
NORM_SLACK = 1.02
ROW_TILE = 256
ML_L = 256
ML_EXT = ML_V_DIM + LANES


def _params(*sem):
    return pltpu.CompilerParams(dimension_semantics=sem, vmem_limit_bytes=VMEM_LIMIT)


def _pick(n, cands):
    for c in cands:
        if n % c == 0:
            return c
    raise ValueError(f"no tile for {n} in {cands}")


def _ada_kernel(cond_ref, wd_ref, wu_ref, b_ref, o_ref):
    c = cond_ref[...]
    a = (c * jax.nn.sigmoid(c)).astype(BF16)
    t = jnp.dot(a, wd_ref[...].astype(BF16), preferred_element_type=F32)
    o_ref[...] = jnp.dot(t.astype(BF16), wu_ref[...].astype(BF16), preferred_element_type=F32) + b_ref[...]


def _ada_all(cond, ada_down, ada_up, ada_b):
    depth, d, r = ada_down.shape
    n6 = ada_up.shape[-1]
    tn = 2048
    return pl.pallas_call(
        _ada_kernel,
        grid=(depth, n6 // tn),
        in_specs=[
            pl.BlockSpec((8, d), lambda l, j: (0, 0)),
            pl.BlockSpec((None, d, r), lambda l, j: (l, 0, 0)),
            pl.BlockSpec((None, r, tn), lambda l, j: (l, 0, j)),
            pl.BlockSpec((None, 1, tn), lambda l, j: (l, 0, j)),
        ],
        out_specs=pl.BlockSpec((None, 8, tn), lambda l, j: (l, 0, j)),
        out_shape=jax.ShapeDtypeStruct((depth, 8, n6), F32),
        compiler_params=_params("arbitrary", "arbitrary"),
        name="ada_mod",
    )(cond, ada_down, ada_up, ada_b.reshape(depth, 1, n6))


def _norm_kernel(x_ref, g_ref, *rest, ks, kc, tm, n_lat):
    o_ref = rest[-1]
    x = x_ref[...].astype(F32)
    y = x * lax.rsqrt(jnp.mean(x * x, axis=-1, keepdims=True) + EPS) * g_ref[...]
    if ks is not None:
        mod_ref = rest[0]
        row = pl.program_id(0) * tm + lax.broadcasted_iota(jnp.int32, (tm, 1), 0)
        is_ctx = row >= n_lat
        scale = jnp.where(is_ctx, mod_ref[1, kc:kc + 1, :], mod_ref[0, kc:kc + 1, :])
        shift = jnp.where(is_ctx, mod_ref[1, ks:ks + 1, :], mod_ref[0, ks:ks + 1, :])
        y = y * (1.0 + scale) + shift
    o_ref[...] = y.astype(o_ref.dtype)


def _rms_rows(x, gain, out_dtype, *, width=None, col_block=0, rows=None, mod=None, ks=None, kc=None,
              n_lat=None):
    t = x.shape[0] if rows is None else rows
    width = x.shape[1] if width is None else width
    tm = ROW_TILE
    in_specs = [pl.BlockSpec((tm, width), lambda i: (i, col_block)),
                pl.BlockSpec((1, width), lambda i: (0, 0))]
    args = [x, gain.reshape(1, width).astype(F32)]
    if mod is not None:
        in_specs.append(pl.BlockSpec((2, N_MOD, width), lambda i: (0, 0, 0)))
        args.append(mod)
    return pl.pallas_call(
        functools.partial(_norm_kernel, ks=ks, kc=kc, tm=tm, n_lat=n_lat),
        grid=(t // tm,),
        in_specs=in_specs,
        out_specs=pl.BlockSpec((tm, width), lambda i: (i, 0)),
        out_shape=jax.ShapeDtypeStruct((t, width), out_dtype),
        compiler_params=_params("parallel"),
        name="rms_rows",
    )(*args)


def _mm_kernel(a_ref, w_ref, *rest, has_bias):
    o_ref = rest[-1]
    acc = jnp.dot(a_ref[...], w_ref[...].astype(BF16), preferred_element_type=F32)
    if has_bias:
        acc = acc + rest[0][...]
    o_ref[...] = acc.astype(o_ref.dtype)


def _wspec(w, layer, block, index):
    if layer is None:
        return pl.BlockSpec(block, index)
    return pl.BlockSpec((None,) + block, lambda *g: (layer,) + index(*g))


def _mm(a, w, out_dtype, *, tn, bias=None, n_out=None, layer=None):
    m, k = a.shape
    n = w.shape[-1] if n_out is None else n_out
    tm = _pick(m, (1280, 1024, 640, 512, 256))
    in_specs = [pl.BlockSpec((tm, k), lambda i, j: (i, 0)),
                _wspec(w, layer, (k, tn), lambda i, j: (0, j))]
    args = [a, w]
    if bias is not None:
        in_specs.append(pl.BlockSpec((1, tn), lambda i, j: (0, j)))
        args.append(bias)
    return pl.pallas_call(
        functools.partial(_mm_kernel, has_bias=bias is not None),
        grid=(m // tm, n // tn),
        in_specs=in_specs,
        out_specs=pl.BlockSpec((tm, tn), lambda i, j: (i, j)),
        out_shape=jax.ShapeDtypeStruct((m, n), out_dtype),
        compiler_params=_params("parallel", "arbitrary"),
        name="mm",
    )(*args)


def _mm_res_kernel(a_ref, w_ref, res_ref, gate_ref, o_ref, *scratch, nk, tm, n_lat):
    part = jnp.dot(a_ref[...], w_ref[...].astype(BF16), preferred_element_type=F32)

    def finish(acc):
        row = pl.program_id(0) * tm + lax.broadcasted_iota(jnp.int32, (tm, 1), 0)
        gate = jnp.where(row >= n_lat, gate_ref[1:2, :], gate_ref[0:1, :])
        o_ref[...] = res_ref[...] + gate * acc

    if nk == 1:
        finish(part)
    else:
        acc_ref = scratch[0]
        kk = pl.program_id(2)

        @pl.when(kk == 0)
        def _():
            acc_ref[...] = part

        @pl.when(jnp.logical_and(kk > 0, kk < nk - 1))
        def _():
            acc_ref[...] += part

        @pl.when(kk == nk - 1)
        def _():
            finish(acc_ref[...] + part)


def _mm_res(a, w, res, gate, n_lat, *, tm, tn, tk, layer=None):
    m, k = a.shape
    n = w.shape[-1]
    nk = k // tk
    scratch = [pltpu.VMEM((tm, tn), F32)] if nk > 1 else []
    return pl.pallas_call(
        functools.partial(_mm_res_kernel, nk=nk, tm=tm, n_lat=n_lat),
        grid=(m // tm, n // tn, nk),
        in_specs=[pl.BlockSpec((tm, tk), lambda i, j, kk: (i, kk)),
                  _wspec(w, layer, (tk, tn), lambda i, j, kk: (kk, j)),
                  pl.BlockSpec((tm, tn), lambda i, j, kk: (i, j)),
                  pl.BlockSpec((2, tn), lambda i, j, kk: (0, j))],
        out_specs=pl.BlockSpec((tm, tn), lambda i, j, kk: (i, j)),
        out_shape=jax.ShapeDtypeStruct((m, n), F32),
        scratch_shapes=scratch,
        compiler_params=_params("parallel", "arbitrary", "arbitrary"),
        name="mm_res",
    )(a, w, res, gate)


def _mm_swiglu_kernel(a_ref, wa_ref, wb_ref, o_ref):
    a = a_ref[...]
    u = jnp.dot(a, wa_ref[...].astype(BF16), preferred_element_type=F32)
    v = jnp.dot(a, wb_ref[...].astype(BF16), preferred_element_type=F32)
    o_ref[...] = (u * jax.nn.sigmoid(u) * v).astype(o_ref.dtype)


def _mm_swiglu(a, w, *, tn, layer=None):
    m, k = a.shape
    f = w.shape[-1] // 2
    nb = f // tn
    tm = _pick(m, (1280, 1024, 640, 512, 256))
    return pl.pallas_call(
        _mm_swiglu_kernel,
        grid=(m // tm, nb),
        in_specs=[pl.BlockSpec((tm, k), lambda i, j: (i, 0)),
                  _wspec(w, layer, (k, tn), lambda i, j: (0, j)),
                  _wspec(w, layer, (k, tn), lambda i, j: (0, j + nb))],
        out_specs=pl.BlockSpec((tm, tn), lambda i, j: (i, j)),
        out_shape=jax.ShapeDtypeStruct((m, f), BF16),
        compiler_params=_params("parallel", "arbitrary"),
        name="mm_swiglu",
    )(a, w, w)


def _stat_update(stat_ref, head, y_bf16):
    yf = y_bf16.astype(F32)
    n2 = jnp.max(jnp.sum(yf * yf, axis=-1, keepdims=True), axis=0, keepdims=True)
    sl = slice(head * 8, (head + 1) * 8)
    stat_ref[sl, :] = jnp.maximum(stat_ref[sl, :], jnp.broadcast_to(n2, (8, LANES)))


def _flash_kernel(q_ref, *rest, groups, dq, dv, nkv, n_kparts, has_into):
    k_refs = rest[:n_kparts]
    vt_ref, qstat_ref, kstat_ref = rest[n_kparts:n_kparts + 3]
    o_ref, m_sc, l_sc, acc_sc, fixed_sc = rest[n_kparts + 3 + int(has_into):]
    h = pl.program_id(0)
    kv = pl.program_id(2)
    nt = (((1,), (1,)), ((), ()))

    @pl.when(kv == 0)
    def _():
        kmax2 = kstat_ref[pl.ds(pl.multiple_of(h * 8, 8), 8), :][0:1, 0:1]
        bound2 = jnp.max(qstat_ref[...]) * jnp.max(kstat_ref[...]) * (FIXED_MARGIN * FIXED_MARGIN)
        use_fixed = bound2 <= FIXED_LIMIT * FIXED_LIMIT
        fixed_sc[0] = use_fixed.astype(jnp.int32)
        acc_sc[...] = jnp.zeros(acc_sc.shape, F32)
        l_sc[...] = jnp.zeros(l_sc.shape, F32)
        for g in range(groups):
            qf = q_ref[:, g * dq:(g + 1) * dq].astype(F32)
            qn2 = lax.dot_general(jnp.ones((8, dq), BF16), (qf * qf).astype(BF16), nt,
                                  preferred_element_type=F32)[0:1, :]
            m_fix = jnp.sqrt(qn2 * kmax2) * FIXED_MARGIN
            m_sc[g] = jnp.where(use_fixed, m_fix, NEG_INF)

    fixed = fixed_sc[0] == 1
    running = jnp.logical_not(fixed)

    def load_k():
        if n_kparts == 1:
            return k_refs[0][...]
        return jnp.concatenate([r[...] for r in k_refs], axis=-1)

    @pl.when(fixed)
    def _():
        k = load_k()
        vt = vt_ref[...]
        for g in range(groups):
            s_t = lax.dot_general(k, q_ref[:, g * dq:(g + 1) * dq], nt, preferred_element_type=F32)
            p_t = jnp.exp2(s_t - m_sc[g])
            l_sc[g] += jnp.sum(p_t, axis=0, keepdims=True)
            acc_sc[g] += jnp.dot(vt, p_t.astype(BF16), preferred_element_type=F32)

    @pl.when(running)
    def _():
        k = load_k()
        vt = vt_ref[...]
        for g in range(groups):
            s_t = lax.dot_general(k, q_ref[:, g * dq:(g + 1) * dq], nt, preferred_element_type=F32)
            m_prev = m_sc[g]
            m_new = jnp.maximum(m_prev, jnp.max(s_t, axis=0, keepdims=True))
            alpha = jnp.exp2(m_prev - m_new)
            p_t = jnp.exp2(s_t - m_new)
            l_sc[g] = alpha * l_sc[g] + jnp.sum(p_t, axis=0, keepdims=True)
            acc_sc[g] = alpha * acc_sc[g] + jnp.dot(vt, p_t.astype(BF16), preferred_element_type=F32)
            m_sc[g] = m_new

    @pl.when(kv == nkv - 1)
    def _():
        for g in range(groups):
            o_ref[:, g * dv:(g + 1) * dv] = (acc_sc[g] / l_sc[g]).T.astype(o_ref.dtype)


def _flash(q, vt, qstat, kstat, *, n_heads_kv, groups, dq, dv, q_row0, nq, kv_row0, nkv_rows, tq, tk, k_cols,
           into=None):
    nkv = nkv_rows // tk
    qb0, kb0 = q_row0 // tq, kv_row0 // tk
    in_specs = [pl.BlockSpec((tq, groups * dq), lambda h, i, j: (qb0 + i, h))]
    args = [q]
    for (arr, width, per_head, col0) in k_cols:
        if per_head:
            in_specs.append(pl.BlockSpec((tk, width), lambda h, i, j, c=col0: (kb0 + j, c + h)))
        else:
            in_specs.append(pl.BlockSpec((tk, width), lambda h, i, j, c=col0: (kb0 + j, c)))
        args.append(arr)
    in_specs.append(pl.BlockSpec((dv, tk), lambda h, i, j: (h, kb0 + j)))
    in_specs.append(pl.BlockSpec(qstat.shape, lambda h, i, j: (0, 0)))
    in_specs.append(pl.BlockSpec(kstat.shape, lambda h, i, j: (0, 0)))
    args += [vt, qstat, kstat]
    aliases = {}
    if into is not None:
        in_specs.append(pl.BlockSpec(memory_space=pl.ANY))
        aliases = {len(args): 0}
        args.append(into)
    return pl.pallas_call(
        functools.partial(_flash_kernel, groups=groups, dq=dq, dv=dv, nkv=nkv, n_kparts=len(k_cols),
                          has_into=into is not None),
        grid=(n_heads_kv, nq // tq, nkv),
        in_specs=in_specs,
        out_specs=pl.BlockSpec((tq, groups * dv), lambda h, i, j: (qb0 + i, h)),
        out_shape=jax.ShapeDtypeStruct((q.shape[0], n_heads_kv * groups * dv), BF16),
        scratch_shapes=[pltpu.VMEM((groups, 1, tq), F32),
                        pltpu.VMEM((groups, 1, tq), F32),
                        pltpu.VMEM((groups, dv, tq), F32),
                        pltpu.SMEM((1,), jnp.int32)],
        input_output_aliases=aliases,
        compiler_params=_params("parallel", "parallel", "arbitrary"),
        name="flash",
    )(*args)


def _attend(q, k_cols, vt, qstat, kstat, *, n_heads_kv, groups, dq, dv, n_lat, n_ctx):
    t = n_lat + n_ctx
    tq = _pick(n_lat, (1024, 512, 256))
    tk = _pick(t, (1280, 640, 256)) if groups > 1 else _pick(t, (3328, 1280, 640, 256))
    common = dict(n_heads_kv=n_heads_kv, groups=groups, dq=dq, dv=dv, k_cols=k_cols)
    o = _flash(q, vt, qstat, kstat, q_row0=0, nq=n_lat, kv_row0=0, nkv_rows=t, tq=tq, tk=tk, **common)
    return _flash(q, vt, qstat, kstat, q_row0=n_lat, nq=n_ctx, kv_row0=n_lat, nkv_rows=n_ctx, tq=n_ctx,
                  tk=n_ctx, into=o, **common)


def _gqa_prep_kernel(qkv_ref, qg_ref, kg_ref, cos_ref, sin_ref, q_ref, k_ref, v_ref):
    cos = cos_ref[...]
    sin = sin_ref[...]
    hd = ATT_HEAD_DIM

    def head(col, gain):
        x = qkv_ref[:, col:col + hd]
        y = x * lax.rsqrt(jnp.mean(x * x, axis=-1, keepdims=True) + EPS) * gain
        return (y * cos + pltpu.roll(y, hd // 2, 1) * sin).astype(BF16)

    for h in range(ATT_HEADS):
        q_ref[:, h * hd:(h + 1) * hd] = head(h * hd, qg_ref[...])
    k0 = ATT_HEADS * hd
    for h in range(ATT_KV_HEADS):
        k_ref[:, h * hd:(h + 1) * hd] = head(k0 + h * hd, kg_ref[...])
    v0 = k0 + ATT_KV_HEADS * hd
    v_ref[...] = qkv_ref[:, v0:v0 + ATT_KV_HEADS * hd].astype(BF16)


def _gqa_prep(qkv, q_gain, k_gain, cos, sin):
    t = qkv.shape[0]
    tm = ROW_TILE
    nq, nk = ATT_HEADS * ATT_HEAD_DIM, ATT_KV_HEADS * ATT_HEAD_DIM
    return pl.pallas_call(
        _gqa_prep_kernel,
        grid=(t // tm,),
        in_specs=[pl.BlockSpec((tm, nq + 2 * nk), lambda i: (i, 0)),
                  pl.BlockSpec((1, ATT_HEAD_DIM), lambda i: (0, 0)),
                  pl.BlockSpec((1, ATT_HEAD_DIM), lambda i: (0, 0)),
                  pl.BlockSpec((tm, ATT_HEAD_DIM), lambda i: (i, 0)),
                  pl.BlockSpec((tm, ATT_HEAD_DIM), lambda i: (i, 0))],
        out_specs=[pl.BlockSpec((tm, nq), lambda i: (i, 0)),
                   pl.BlockSpec((tm, nk), lambda i: (i, 0)),
                   pl.BlockSpec((tm, nk), lambda i: (i, 0))],
        out_shape=[jax.ShapeDtypeStruct((t, nq), BF16),
                   jax.ShapeDtypeStruct((t, nk), BF16),
                   jax.ShapeDtypeStruct((t, nk), BF16)],
        compiler_params=_params("parallel"),
        name="gqa_prep",
    )(qkv, q_gain.reshape(1, -1), k_gain.reshape(1, -1), cos, sin)


def _gqa_mixer(h, w_in_all, layer, q_gain, k_gain, cos, sin, n_lat, n_ctx):
    hd = ATT_HEAD_DIM
    qkv = _mm(h, w_in_all, F32, tn=512, layer=layer)
    gq = q_gain.astype(F32) * (hd ** -0.5 * LOG2E)
    gk = k_gain.astype(F32)
    q, k, v = _gqa_prep(qkv, gq, gk, cos, sin)
    qstat = jnp.full((ATT_HEADS * 8, LANES), hd * NORM_SLACK, F32) * jnp.max(gq * gq)
    kstat = jnp.full((ATT_KV_HEADS * 8, LANES), hd * NORM_SLACK, F32) * jnp.max(gk * gk)
    return _attend(q, [(k, hd, True, 0)], v.T, qstat, kstat, n_heads_kv=ATT_KV_HEADS,
                   groups=ATT_HEADS // ATT_KV_HEADS, dq=hd, dv=hd, n_lat=n_lat, n_ctx=n_ctx)


def _swap_halves_64(x):
    lane = lax.broadcasted_iota(jnp.int32, x.shape, 1)
    return jnp.where(lane % MLA_ROPE < MLA_ROPE // 2, pltpu.roll(x, LANES - MLA_ROPE // 2, 1),
                     pltpu.roll(x, MLA_ROPE // 2, 1))


def _mla_prep_kernel(qn_ref, qr_ref, kr_ref, kn_ref, cos_ref, sin_ref, qf_ref, kk_ref, qstat_ref, kstat_ref, *,
                     qscale):
    cos = cos_ref[...]
    sin = sin_ref[...]
    lane = lax.broadcasted_iota(jnp.int32, cos.shape, 1)
    low = lane < MLA_ROPE
    wq = MLA_NOPE + LANES

    @pl.when(pl.program_id(0) == 0)
    def _():
        qstat_ref[...] = jnp.zeros(qstat_ref.shape, F32)
        kstat_ref[...] = jnp.zeros(kstat_ref.shape, F32)

    kr = jnp.where(low, kr_ref[...], 0.0)
    ky = kr * cos + _swap_halves_64(kr) * sin
    kk = (ky + pltpu.roll(ky, MLA_ROPE, 1)).astype(BF16)
    kk_ref[...] = kk
    for j in range(MLA_HEADS // 2):
        x = qr_ref[:, j * LANES:(j + 1) * LANES]
        y = (x * cos + _swap_halves_64(x) * sin) * qscale
        for h, yr in ((2 * j, jnp.where(low, y, 0.0)), (2 * j + 1, jnp.where(low, 0.0, y))):
            qh = jnp.concatenate([qn_ref[:, h * MLA_NOPE:(h + 1) * MLA_NOPE] * qscale, yr], axis=-1).astype(BF16)
            qf_ref[:, h * wq:(h + 1) * wq] = qh
            _stat_update(qstat_ref, h, qh)
            _stat_update(kstat_ref, h, jnp.concatenate([kn_ref[:, h * MLA_NOPE:(h + 1) * MLA_NOPE], kk], axis=-1))


def _mla_prep(qp, t_in, kvp, cos, sin):
    t = qp.shape[0]
    tm = ROW_TILE
    nn, nr = MLA_HEADS * MLA_NOPE, MLA_HEADS * MLA_ROPE
    qscale = (MLA_NOPE + MLA_ROPE) ** -0.5 * LOG2E
    kr_block = (MLA_Q_RANK + MLA_KV_RANK) // LANES
    return pl.pallas_call(
        functools.partial(_mla_prep_kernel, qscale=qscale),
        grid=(t // tm,),
        in_specs=[pl.BlockSpec((tm, nn), lambda i: (i, 0)),
                  pl.BlockSpec((tm, nr), lambda i: (i, nn // nr)),
                  pl.BlockSpec((tm, LANES), lambda i: (i, kr_block)),
                  pl.BlockSpec((tm, nn), lambda i: (i, 0)),
                  pl.BlockSpec((tm, LANES), lambda i: (i, 0)),
                  pl.BlockSpec((tm, LANES), lambda i: (i, 0))],
        out_specs=[pl.BlockSpec((tm, MLA_HEADS * (MLA_NOPE + LANES)), lambda i: (i, 0)),
                   pl.BlockSpec((tm, LANES), lambda i: (i, 0)),
                   pl.BlockSpec((MLA_HEADS * 8, LANES), lambda i: (0, 0)),
                   pl.BlockSpec((MLA_HEADS * 8, LANES), lambda i: (0, 0))],
        out_shape=[jax.ShapeDtypeStruct((t, MLA_HEADS * (MLA_NOPE + LANES)), BF16),
                   jax.ShapeDtypeStruct((t, LANES), BF16),
                   jax.ShapeDtypeStruct((MLA_HEADS * 8, LANES), F32),
                   jax.ShapeDtypeStruct((MLA_HEADS * 8, LANES), F32)],
        compiler_params=_params("arbitrary"),
        name="mla_prep",
    )(qp, qp, t_in, kvp, cos, sin)


def _mla_mixer(h, w_in, q_gain, kv_gain, w_qb, w_kvb, cos, sin, n_lat, n_ctx):
    pad = -w_in.shape[1] % 512
    t_in = _mm(h, jnp.pad(w_in, ((0, 0), (0, pad))).astype(BF16), F32, tn=512)
    cq = _rms_rows(t_in, q_gain, BF16, width=MLA_Q_RANK, col_block=0)
    ckv = _rms_rows(t_in, kv_gain, BF16, width=MLA_KV_RANK, col_block=MLA_Q_RANK // MLA_KV_RANK)
    wq = w_qb.reshape(MLA_Q_RANK, MLA_HEADS, MLA_NOPE + MLA_ROPE)
    wq = jnp.concatenate([wq[:, :, :MLA_NOPE].reshape(MLA_Q_RANK, -1),
                          wq[:, :, MLA_NOPE:].reshape(MLA_Q_RANK, -1)], axis=1).astype(BF16)
    wkv = w_kvb.reshape(MLA_KV_RANK, MLA_HEADS, MLA_NOPE + MLA_V)
    wk = wkv[:, :, :MLA_NOPE].reshape(MLA_KV_RANK, -1).astype(BF16)
    wv_t = wkv[:, :, MLA_NOPE:].reshape(MLA_KV_RANK, -1).T.astype(BF16)
    qp = _mm(cq, wq, F32, tn=512)
    kvp = _mm(ckv, wk, BF16, tn=512)
    vt = _mm(wv_t, ckv.T, BF16, tn=_pick(ckv.shape[0], (1280, 640, 256)))
    qf, kk, qstat, kstat = _mla_prep(qp, t_in, kvp, cos, sin)
    return _attend(qf, [(kvp, MLA_NOPE, True, 0), (kk, LANES, False, 0)], vt, qstat, kstat,
                   n_heads_kv=MLA_HEADS, groups=1, dq=MLA_NOPE + LANES, dv=MLA_V,
                   n_lat=n_lat, n_ctx=n_ctx)


def _log_sigmoid(x):
    return jnp.minimum(x, 0.0) - jnp.log1p(jnp.exp(-jnp.abs(x)))


def _mlstm_kernel(q_ref, k_ref, v_ref, gc_ref, gr_ref, h_ref, c_sc, m_sc, *, rev):
    L, dk, dvv = ML_L, ML_QK_DIM, ML_V_DIM
    step = pl.program_id(0)

    @pl.when(step == 0)
    def _():
        c_sc[...] = jnp.zeros(c_sc.shape, F32)
        m_sc[...] = jnp.zeros(m_sc.shape, F32)

    t_idx = lax.broadcasted_iota(jnp.int32, (L, L), 0)
    s_idx = lax.broadcasted_iota(jnp.int32, (L, L), 1)
    mask = (s_idx >= t_idx) if rev else (s_idx <= t_idx)
    tri = mask.astype(F32)
    gc = gc_ref[...]
    gr = gr_ref[...]
    b_col = jnp.dot(tri, _log_sigmoid(gc), preferred_element_type=F32, precision=lax.Precision.HIGHEST)
    b_row = lax.dot_general(_log_sigmoid(gr), tri, (((1,), (1,)), ((), ())),
                            preferred_element_type=F32, precision=lax.Precision.HIGHEST)
    last = 0 if rev else L - 1
    i_base, f_base = (2 * ML_HEADS, 3 * ML_HEADS) if rev else (0, ML_HEADS)
    lane_e = lax.broadcasted_iota(jnp.int32, (L, LANES), 1)
    ones_col = jnp.where(lane_e == 0, 1.0, 0.0).astype(BF16)

    for hh in range(ML_HEADS):
        ci, cf = i_base + hh, f_base + hh
        bc = b_col[:, cf:cf + 1]
        br = b_row[cf:cf + 1, :]
        lic = gc[:, ci:ci + 1]
        lir = gr[ci:ci + 1, :]
        g = bc[last:last + 1, :]
        m_prev = m_sc[hh][0:1, 0:1]
        q = q_ref[:, hh * dk:(hh + 1) * dk].astype(BF16)
        kf = k_ref[:, hh * dk:(hh + 1) * dk] * (dk ** -0.5)
        v_ext = jnp.concatenate([v_ref[:, hh * dvv:(hh + 1) * dvv].astype(BF16), ones_col], axis=-1)

        d_log = jnp.where(mask, bc - br + lir, NEG_INF)
        inter = bc + m_prev
        m_t = jnp.maximum(inter, jnp.max(d_log, axis=-1, keepdims=True))
        w_intra = jnp.exp(d_log - m_t)
        w_inter = jnp.exp(inter - m_t)
        s = lax.dot_general(q, kf.astype(BF16), (((1,), (1,)), ((), ())), preferred_element_type=F32) * w_intra
        c_old = c_sc[hh]
        ext = w_inter * jnp.dot(q, c_old.astype(BF16), preferred_element_type=F32) \
            + jnp.dot(s.astype(BF16), v_ext, preferred_element_type=F32)
        den = ext[:, dvv:dvv + 1]
        h_ref[:, hh * dvv:(hh + 1) * dvv] = ext[:, :dvv] / jnp.maximum(jnp.abs(den), jnp.exp(-m_t))

        a = g - bc + lic
        m_new = jnp.maximum(g + m_prev, jnp.max(a, axis=0, keepdims=True))
        w_s = jnp.exp(a - m_new)
        decay = jnp.exp(g + m_prev - m_new)
        kw = (kf * w_s).astype(BF16)
        c_sc[hh] = decay * c_old + lax.dot_general(kw, v_ext, (((0,), (0,)), ((), ())),
                                                   preferred_element_type=F32)
        m_sc[hh] = jnp.broadcast_to(m_new, m_sc.shape[1:])


def _mlstm_scan(qkvo, gates, gates_t, n_lat, n_ctx, rev):
    t = n_lat + n_ctx
    L = ML_L
    nlat, nctx = n_lat // L, n_ctx // L
    nqk, nv = ML_HEADS * ML_QK_DIM, ML_HEADS * ML_V_DIM

    def chunk(c):
        if rev:
            return nlat + nctx - 1 - c
        return jnp.where(c < nctx, nlat + c, c - nctx)

    return pl.pallas_call(
        functools.partial(_mlstm_kernel, rev=rev),
        grid=(nlat + nctx,),
        in_specs=[pl.BlockSpec((L, nqk), lambda c: (chunk(c), 0)),
                  pl.BlockSpec((L, nqk), lambda c: (chunk(c), 1)),
                  pl.BlockSpec((L, nv), lambda c: (chunk(c), (2 * nqk) // nv)),
                  pl.BlockSpec((L, LANES), lambda c: (chunk(c), 0)),
                  pl.BlockSpec((4 * ML_HEADS, L), lambda c: (0, chunk(c)))],
        out_specs=pl.BlockSpec((L, nv), lambda c: (chunk(c), 0)),
        out_shape=jax.ShapeDtypeStruct((t, nv), F32),
        scratch_shapes=[pltpu.VMEM((ML_HEADS, ML_QK_DIM, ML_EXT), F32),
                        pltpu.VMEM((ML_HEADS, 8, LANES), F32)],
        compiler_params=_params("arbitrary"),
        name="mlstm_rev" if rev else "mlstm_fwd",
    )(qkvo, qkvo, qkvo, gates, gates_t)


def _mlstm_finish_kernel(hf_ref, hb_ref, o_ref, g_ref, y_ref):
    dvv = ML_V_DIM
    for hh in range(ML_HEADS):
        sl = slice(hh * dvv, (hh + 1) * dvv)
        h = hf_ref[:, sl] + hb_ref[:, sl]
        hn = h * lax.rsqrt(jnp.mean(h * h, axis=-1, keepdims=True) + EPS) * g_ref[:, sl]
        y_ref[:, sl] = (hn * jax.nn.sigmoid(o_ref[:, sl])).astype(BF16)


def _mlstm_finish(h_f, h_b, qkvo, h_gain):
    t, nv = h_f.shape
    tm = ROW_TILE
    o_block = (2 * ML_HEADS * ML_QK_DIM + nv) // nv
    return pl.pallas_call(
        _mlstm_finish_kernel,
        grid=(t // tm,),
        in_specs=[pl.BlockSpec((tm, nv), lambda i: (i, 0)),
                  pl.BlockSpec((tm, nv), lambda i: (i, 0)),
                  pl.BlockSpec((tm, nv), lambda i: (i, o_block)),
                  pl.BlockSpec((1, nv), lambda i: (0, 0))],
        out_specs=pl.BlockSpec((tm, nv), lambda i: (i, 0)),
        out_shape=jax.ShapeDtypeStruct((t, nv), BF16),
        compiler_params=_params("parallel"),
        name="mlstm_finish",
    )(h_f, h_b, qkvo, h_gain.reshape(1, nv))


def _mlstm_mixer(h, w_in_all, layer, gate_b, h_gain, n_lat, n_ctx):
    n_main = 2 * ML_HEADS * ML_QK_DIM + 2 * ML_HEADS * ML_V_DIM
    n_g = 4 * ML_HEADS
    qkvo = _mm(h, w_in_all, F32, tn=512, n_out=n_main, layer=layer)
    w_g = jnp.pad(w_in_all[layer, :, n_main:], ((0, 0), (0, LANES - n_g))).astype(BF16)
    b_g = jnp.pad(gate_b.astype(F32), (0, LANES - n_g)).reshape(1, LANES)
    gates = _mm(h, w_g, F32, tn=LANES, bias=b_g)
    gates_t = gates[:, :n_g].T
    h_f = _mlstm_scan(qkvo, gates, gates_t, n_lat, n_ctx, False)
    h_b = _mlstm_scan(qkvo, gates, gates_t, n_lat, n_ctx, True)
    return _mlstm_finish(h_f, h_b, qkvo, h_gain)


def _rope_tables(n_lat, n_ctx, rot_dim):
    rows = n_lat // GRID_W
    row = jnp.repeat(jnp.arange(rows, dtype=F32), GRID_W)
    col = jnp.tile(jnp.arange(GRID_W, dtype=F32), rows)
    n_freq = rot_dim // 4
    inv_freq = ROPE_THETA ** (-jnp.arange(n_freq, dtype=F32) / n_freq)
    ang = jnp.concatenate([row[:, None] * inv_freq, col[:, None] * inv_freq], axis=-1)
    cos, sin = jnp.cos(ang), jnp.sin(ang)
    reps = LANES // rot_dim
    cos_t = jnp.tile(jnp.concatenate([cos, cos], axis=-1), (1, reps))
    sin_t = jnp.tile(jnp.concatenate([-sin, sin], axis=-1), (1, reps))
    cos_t = jnp.concatenate([cos_t, jnp.ones((n_ctx, LANES), F32)], axis=0)
    sin_t = jnp.concatenate([sin_t, jnp.zeros((n_ctx, LANES), F32)], axis=0)
    return cos_t, sin_t


def kernel(x, c, ctx, c_ctx, ada_down, ada_up, ada_b, norm_mix, norm_ffn, ffn_in, ffn_out, att_in, att_qnorm,
           att_knorm, att_out, ml_in, ml_gate_b, ml_hnorm, ml_out, mla_in, mla_qnorm, mla_kvnorm, mla_qb,
           mla_kvb, mla_out, final_norm):
    b, n_lat, d = x.shape
    n_ctx = ctx.shape[1]
    depth = ada_down.shape[0]
    assert b == 1 and d == D_MODEL and n_lat % ROW_TILE == 0 and n_ctx % ML_L == 0 and n_lat % ML_L == 0

    cos_a, sin_a = _rope_tables(n_lat, n_ctx, ATT_HEAD_DIM)
    cos_m, sin_m = _rope_tables(n_lat, n_ctx, MLA_ROPE)

    cond = jnp.concatenate([c.reshape(1, d), c_ctx.reshape(1, d), jnp.zeros((6, d), F32)], axis=0)
    mods = _ada_all(cond, ada_down, ada_up, ada_b).reshape(depth, 8, N_MOD, d)

    xs = jnp.concatenate([x[0], ctx[0]], axis=0)
    t = n_lat + n_ctx
    tm_out = _pick(t, (1280, 1024, 640, 512, 256))
    tm_ffn = _pick(t, (640, 512, 256))
    ffn_out_bf16 = ffn_out.astype(BF16)

    for i in range(depth):
        kind, j = i % 3, i // 3
        mod = mods[i, :2]
        h = _rms_rows(xs, norm_mix[i], BF16, mod=mod, ks=0, kc=1, n_lat=n_lat)
        if kind == 0:
            o = _gqa_mixer(h, att_in, j, att_qnorm[j], att_knorm[j], cos_a, sin_a, n_lat, n_ctx)
            w_o = att_out
        elif kind == 1:
            o = _mlstm_mixer(h, ml_in, j, ml_gate_b[j], ml_hnorm[j], n_lat, n_ctx)
            w_o = ml_out
        else:
            o = _mla_mixer(h, mla_in[j], mla_qnorm[j], mla_kvnorm[j], mla_qb[j], mla_kvb[j], cos_m, sin_m,
                           n_lat, n_ctx)
            w_o = mla_out
        xs = _mm_res(o, w_o.astype(BF16), xs, mod[:, 2], n_lat, tm=tm_out, tn=512, tk=w_o.shape[1], layer=j)
        h2 = _rms_rows(xs, norm_ffn[i], BF16, mod=mod, ks=3, kc=4, n_lat=n_lat)
        act = _mm_swiglu(h2, ffn_in, tn=256, layer=i)
        xs = _mm_res(act, ffn_out_bf16, xs, mod[:, 5], n_lat, tm=tm_ffn, tn=1024, tk=D_FF // 2, layer=i)

    out = _rms_rows(xs, final_norm, F32, rows=n_lat)
    return out.reshape(1, n_lat, d)
```

```python
import functools

import jax
import jax.numpy as jnp
from jax import lax
from jax.experimental import pallas as pl
from jax.experimental.pallas import tpu as pltpu

F32 = jnp.float32
BF16 = jnp.bfloat16

D_MODEL = 4096
GRID_W = 64
EPS = 1e-6
ROPE_THETA = 10000.0
ADA_RANK = 256
N_MOD = 6
D_FF = -(-8 * D_MODEL // (3 * 256)) * 256
ATT_HEAD_DIM = 128
ATT_HEADS = D_MODEL // ATT_HEAD_DIM
ATT_KV_HEADS = ATT_HEADS // 4
ML_HEADS = 8
ML_QK_DIM = D_MODEL // 2 // ML_HEADS
ML_V_DIM = D_MODEL // ML_HEADS
MLA_HEADS = D_MODEL // 128
MLA_Q_RANK = D_MODEL // 4
MLA_KV_RANK = 512
MLA_NOPE = 128
MLA_ROPE = 64
MLA_V = 128

LANES = 128
VMEM_LIMIT = 56 * 1024 * 1024
LOG2E = 1.4426950408889634
NEG_INF = float("-inf")
FIXED_MARGIN = 1.02
FIXED_LIMIT = 60.0
NORM_SLACK = 1.02
ROW_TILE = 256
NORM_ROWS = 16
ML_L = 256
ML_EXT = ML_V_DIM + LANES


def _params(*sem):
    return pltpu.CompilerParams(dimension_semantics=sem, vmem_limit_bytes=VMEM_LIMIT)


def _pick(n, cands):
    for c in cands:
        if n % c == 0:
            return c
    raise ValueError(f"no tile for {n} in {cands}")


def _ada_kernel(cond_ref, wd_ref, wu_ref, b_ref, o_ref):
    c = cond_ref[...]
    a = (c * jax.nn.sigmoid(c)).astype(BF16)
    t = jnp.dot(a, wd_ref[...].astype(BF16), preferred_element_type=F32)
    o_ref[...] = jnp.dot(t.astype(BF16), wu_ref[...].astype(BF16), preferred_element_type=F32) + b_ref[...]


def _ada_all(cond, ada_down, ada_up, ada_b):
    depth, d, r = ada_down.shape
    n6 = ada_up.shape[-1]
    tn = 2048
    return pl.pallas_call(
        _ada_kernel,
        grid=(depth, n6 // tn),
        in_specs=[
            pl.BlockSpec((8, d), lambda l, j: (0, 0)),
            pl.BlockSpec((None, d, r), lambda l, j: (l, 0, 0)),
            pl.BlockSpec((None, r, tn), lambda l, j: (l, 0, j)),
            pl.BlockSpec((None, 1, tn), lambda l, j: (l, 0, j)),
        ],
        out_specs=pl.BlockSpec((None, 8, tn), lambda l, j: (l, 0, j)),
        out_shape=jax.ShapeDtypeStruct((depth, 8, n6), F32),
        compiler_params=_params("arbitrary", "arbitrary"),
        name="ada_mod",
    )(cond, ada_down, ada_up, ada_b.reshape(depth, 1, n6))


def _norm_kernel(x_ref, g_ref, *rest, ks, kc):
    o_ref = rest[-1]
    gain = g_ref[...]
    shift = None
    if ks is not None:
        mod_ref = rest[0]
        gain = gain * (1.0 + mod_ref[kc:kc + 1, :])
        shift = mod_ref[ks:ks + 1, :]

    def group(r, carry):
        rows = pl.ds(pl.multiple_of(r * NORM_ROWS, NORM_ROWS), NORM_ROWS)
        x = x_ref[rows, :].astype(F32)
        y = x * lax.rsqrt(jnp.mean(x * x, axis=-1, keepdims=True) + EPS) * gain
        if shift is not None:
            y = y + shift
        o_ref[rows, :] = y.astype(o_ref.dtype)
        return carry

    lax.fori_loop(0, x_ref.shape[0] // NORM_ROWS, group, 0, unroll=2)


def _rms_rows(x, gain, out_dtype, *, width=None, col_block=0, rows=None, mod=None, ks=None, kc=None,
              n_lat=None):
    t = x.shape[0] if rows is None else rows
    width = x.shape[1] if width is None else width
    tm = ROW_TILE
    in_specs = [pl.BlockSpec((tm, width), lambda i: (i, col_block)),
                pl.BlockSpec((1, width), lambda i: (0, 0))]
    args = [x, gain.reshape(1, width).astype(F32)]
    if mod is not None:
        nb_lat = n_lat // tm
        in_specs.append(pl.BlockSpec((None, N_MOD, width), lambda i: (jnp.where(i >= nb_lat, 1, 0), 0, 0)))
        args.append(mod)
    return pl.pallas_call(
        functools.partial(_norm_kernel, ks=ks, kc=kc),
        grid=(t // tm,),
        in_specs=in_specs,
        out_specs=pl.BlockSpec((tm, width), lambda i: (i, 0)),
        out_shape=jax.ShapeDtypeStruct((t, width), out_dtype),
        compiler_params=_params("parallel"),
        name="rms_rows",
    )(*args)


def _mm_kernel(a_ref, w_ref, *rest, has_bias):
    o_ref = rest[-1]
    acc = jnp.dot(a_ref[...], w_ref[...].astype(BF16), preferred_element_type=F32)
    if has_bias:
        acc = acc + rest[0][...]
    o_ref[...] = acc.astype(o_ref.dtype)


def _wspec(w, layer, block, index):
    if layer is None:
        return pl.BlockSpec(block, index)
    return pl.BlockSpec((None,) + block, lambda *g: (layer,) + index(*g))


def _mm(a, w, out_dtype, *, tn, bias=None, n_out=None, layer=None):
    m, k = a.shape
    n = w.shape[-1] if n_out is None else n_out
    tm = _pick(m, (1280, 1024, 640, 512, 256))
    in_specs = [pl.BlockSpec((tm, k), lambda i, j: (i, 0)),
                _wspec(w, layer, (k, tn), lambda i, j: (0, j))]
    args = [a, w]
    if bias is not None:
        in_specs.append(pl.BlockSpec((1, tn), lambda i, j: (0, j)))
        args.append(bias)
    return pl.pallas_call(
        functools.partial(_mm_kernel, has_bias=bias is not None),
        grid=(m // tm, n // tn),
        in_specs=in_specs,
        out_specs=pl.BlockSpec((tm, tn), lambda i, j: (i, j)),
        out_shape=jax.ShapeDtypeStruct((m, n), out_dtype),
        compiler_params=_params("parallel", "arbitrary"),
        name="mm",
    )(*args)


def _mm_res_kernel(a_ref, w_ref, res_ref, gate_ref, o_ref, *scratch, nk, tm, n_lat):
    part = jnp.dot(a_ref[...], w_ref[...].astype(BF16), preferred_element_type=F32)

    def finish(acc):
        row = pl.program_id(0) * tm + lax.broadcasted_iota(jnp.int32, (tm, 1), 0)
        gate = jnp.where(row >= n_lat, gate_ref[1:2, :], gate_ref[0:1, :])
        o_ref[...] = res_ref[...] + gate * acc

    if nk == 1:
        finish(part)
    else:
        acc_ref = scratch[0]
        kk = pl.program_id(2)

        @pl.when(kk == 0)
        def _():
            acc_ref[...] = part

        @pl.when(jnp.logical_and(kk > 0, kk < nk - 1))
        def _():
            acc_ref[...] += part

        @pl.when(kk == nk - 1)
        def _():
            finish(acc_ref[...] + part)


def _mm_res(a, w, res, gate, n_lat, *, tm, tn, tk, layer=None):
    m, k = a.shape
    n = w.shape[-1]
    nk = k // tk
    scratch = [pltpu.VMEM((tm, tn), F32)] if nk > 1 else []
    return pl.pallas_call(
        functools.partial(_mm_res_kernel, nk=nk, tm=tm, n_lat=n_lat),
        grid=(m // tm, n // tn, nk),
        in_specs=[pl.BlockSpec((tm, tk), lambda i, j, kk: (i, kk)),
                  _wspec(w, layer, (tk, tn), lambda i, j, kk: (kk, j)),
                  pl.BlockSpec((tm, tn), lambda i, j, kk: (i, j)),
                  pl.BlockSpec((2, tn), lambda i, j, kk: (0, j))],
        out_specs=pl.BlockSpec((tm, tn), lambda i, j, kk: (i, j)),
        out_shape=jax.ShapeDtypeStruct((m, n), F32),
        scratch_shapes=scratch,
        compiler_params=_params("parallel", "arbitrary", "arbitrary"),
        name="mm_res",
    )(a, w, res, gate)


def _mm_swiglu_kernel(a_ref, wa_ref, wb_ref, o_ref):
    a = a_ref[...]
    u = jnp.dot(a, wa_ref[...].astype(BF16), preferred_element_type=F32)
    v = jnp.dot(a, wb_ref[...].astype(BF16), preferred_element_type=F32)
    o_ref[...] = (u * jax.nn.sigmoid(u) * v).astype(o_ref.dtype)


def _mm_swiglu(a, w, *, tn, layer=None):
    m, k = a.shape
    f = w.shape[-1] // 2
    nb = f // tn
    tm = _pick(m, (1280, 1024, 640, 512, 256))
    return pl.pallas_call(
        _mm_swiglu_kernel,
        grid=(m // tm, nb),
        in_specs=[pl.BlockSpec((tm, k), lambda i, j: (i, 0)),
                  _wspec(w, layer, (k, tn), lambda i, j: (0, j)),
                  _wspec(w, layer, (k, tn), lambda i, j: (0, j + nb))],
        out_specs=pl.BlockSpec((tm, tn), lambda i, j: (i, j)),
        out_shape=jax.ShapeDtypeStruct((m, f), BF16),
        compiler_params=_params("parallel", "arbitrary"),
        name="mm_swiglu",
    )(a, w, w)


def _stat_update(stat_ref, head, y_bf16):
    yf = y_bf16.astype(F32)
    n2 = jnp.max(jnp.sum(yf * yf, axis=-1, keepdims=True), axis=0, keepdims=True)
    sl = slice(head * 8, (head + 1) * 8)
    stat_ref[sl, :] = jnp.maximum(stat_ref[sl, :], jnp.broadcast_to(n2, (8, LANES)))


def _flash_kernel(q_ref, *rest, groups, dq, dv, nkv, n_kparts, has_into):
    k_refs = rest[:n_kparts]
    vt_ref, qstat_ref, kstat_ref = rest[n_kparts:n_kparts + 3]
    o_ref, m_sc, l_sc, acc_sc, fixed_sc = rest[n_kparts + 3 + int(has_into):]
    h = pl.program_id(0)
    kv = pl.program_id(2)
    nt = (((1,), (1,)), ((), ()))

    @pl.when(kv == 0)
    def _():
        kmax2 = kstat_ref[pl.ds(pl.multiple_of(h * 8, 8), 8), :][0:1, 0:1]
        bound2 = jnp.max(qstat_ref[...]) * jnp.max(kstat_ref[...]) * (FIXED_MARGIN * FIXED_MARGIN)
        use_fixed = bound2 <= FIXED_LIMIT * FIXED_LIMIT
        fixed_sc[0] = use_fixed.astype(jnp.int32)
        acc_sc[...] = jnp.zeros(acc_sc.shape, F32)
        l_sc[...] = jnp.zeros(l_sc.shape, F32)
        for g in range(groups):
            qf = q_ref[:, g * dq:(g + 1) * dq].astype(F32)
            qn2 = lax.dot_general(jnp.ones((8, dq), BF16), (qf * qf).astype(BF16), nt,
                                  preferred_element_type=F32)[0:1, :]
            m_fix = jnp.sqrt(qn2 * kmax2) * FIXED_MARGIN
            m_sc[g] = jnp.where(use_fixed, m_fix, NEG_INF)

    fixed = fixed_sc[0] == 1
    running = jnp.logical_not(fixed)

    def load_k():
        if n_kparts == 1:
            return k_refs[0][...]
        return jnp.concatenate([r[...] for r in k_refs], axis=-1)

    @pl.when(fixed)
    def _():
        k = load_k()
        vt = vt_ref[...]
        for g in range(groups):
            s_t = lax.dot_general(k, q_ref[:, g * dq:(g + 1) * dq], nt, preferred_element_type=F32)
            p_t = jnp.exp2(s_t - m_sc[g])
            l_sc[g] += jnp.sum(p_t, axis=0, keepdims=True)
            acc_sc[g] += jnp.dot(vt, p_t.astype(BF16), preferred_element_type=F32)

    @pl.when(running)
    def _():
        k = load_k()
        vt = vt_ref[...]
        for g in range(groups):
            s_t = lax.dot_general(k, q_ref[:, g * dq:(g + 1) * dq], nt, preferred_element_type=F32)
            m_prev = m_sc[g]
            m_new = jnp.maximum(m_prev, jnp.max(s_t, axis=0, keepdims=True))
            alpha = jnp.exp2(m_prev - m_new)
            p_t = jnp.exp2(s_t - m_new)
            l_sc[g] = alpha * l_sc[g] + jnp.sum(p_t, axis=0, keepdims=True)
            acc_sc[g] = alpha * acc_sc[g] + jnp.dot(vt, p_t.astype(BF16), preferred_element_type=F32)
            m_sc[g] = m_new

    @pl.when(kv == nkv - 1)
    def _():
        for g in range(groups):
            o_ref[:, g * dv:(g + 1) * dv] = (acc_sc[g] / l_sc[g]).T.astype(o_ref.dtype)


def _flash(q, vt, qstat, kstat, *, n_heads_kv, groups, dq, dv, q_row0, nq, kv_row0, nkv_rows, tq, tk, k_cols,
           into=None):
    nkv = nkv_rows // tk
    qb0, kb0 = q_row0 // tq, kv_row0 // tk
    in_specs = [pl.BlockSpec((tq, groups * dq), lambda h, i, j: (qb0 + i, h))]
    args = [q]
    for (arr, width, per_head, col0) in k_cols:
        if per_head:
            in_specs.append(pl.BlockSpec((tk, width), lambda h, i, j, c=col0: (kb0 + j, c + h)))
        else:
            in_specs.append(pl.BlockSpec((tk, width), lambda h, i, j, c=col0: (kb0 + j, c)))
        args.append(arr)
    in_specs.append(pl.BlockSpec((dv, tk), lambda h, i, j: (h, kb0 + j)))
    in_specs.append(pl.BlockSpec(qstat.shape, lambda h, i, j: (0, 0)))
    in_specs.append(pl.BlockSpec(kstat.shape, lambda h, i, j: (0, 0)))
    args += [vt, qstat, kstat]
    aliases = {}
    if into is not None:
        in_specs.append(pl.BlockSpec(memory_space=pl.ANY))
        aliases = {len(args): 0}
        args.append(into)
    return pl.pallas_call(
        functools.partial(_flash_kernel, groups=groups, dq=dq, dv=dv, nkv=nkv, n_kparts=len(k_cols),
                          has_into=into is not None),
        grid=(n_heads_kv, nq // tq, nkv),
        in_specs=in_specs,
        out_specs=pl.BlockSpec((tq, groups * dv), lambda h, i, j: (qb0 + i, h)),
        out_shape=jax.ShapeDtypeStruct((q.shape[0], n_heads_kv * groups * dv), BF16),
        scratch_shapes=[pltpu.VMEM((groups, 1, tq), F32),
                        pltpu.VMEM((groups, 1, tq), F32),
                        pltpu.VMEM((groups, dv, tq), F32),
                        pltpu.SMEM((1,), jnp.int32)],
        input_output_aliases=aliases,
        compiler_params=_params("parallel", "parallel", "arbitrary"),
        name="flash",
    )(*args)


def _attend(q, k_cols, vt, qstat, kstat, *, n_heads_kv, groups, dq, dv, n_lat, n_ctx):
    t = n_lat + n_ctx
    tq = _pick(n_lat, (1024, 512, 256))
    tk = _pick(t, (3328, 1280, 640, 256))
    common = dict(n_heads_kv=n_heads_kv, groups=groups, dq=dq, dv=dv, k_cols=k_cols)
    o = _flash(q, vt, qstat, kstat, q_row0=0, nq=n_lat, kv_row0=0, nkv_rows=t, tq=tq, tk=tk, **common)
    return _flash(q, vt, qstat, kstat, q_row0=n_lat, nq=n_ctx, kv_row0=n_lat, nkv_rows=n_ctx, tq=n_ctx,
                  tk=n_ctx, into=o, **common)


def _gqa_prep_kernel(qkv_ref, qg_ref, kg_ref, cos_ref, sin_ref, q_ref, k_ref, v_ref):
    cos = cos_ref[...]
    sin = sin_ref[...]
    hd = ATT_HEAD_DIM

    def head(col, gain):
        x = qkv_ref[:, col:col + hd]
        y = x * lax.rsqrt(jnp.mean(x * x, axis=-1, keepdims=True) + EPS) * gain
        return (y * cos + pltpu.roll(y, hd // 2, 1) * sin).astype(BF16)

    for h in range(ATT_HEADS):
        q_ref[:, h * hd:(h + 1) * hd] = head(h * hd, qg_ref[...])
    k0 = ATT_HEADS * hd
    for h in range(ATT_KV_HEADS):
        k_ref[:, h * hd:(h + 1) * hd] = head(k0 + h * hd, kg_ref[...])
    v0 = k0 + ATT_KV_HEADS * hd
    v_ref[...] = qkv_ref[:, v0:v0 + ATT_KV_HEADS * hd].astype(BF16)


def _gqa_prep(qkv, q_gain, k_gain, cos, sin):
    t = qkv.shape[0]
    tm = ROW_TILE
    nq, nk = ATT_HEADS * ATT_HEAD_DIM, ATT_KV_HEADS * ATT_HEAD_DIM
    return pl.pallas_call(
        _gqa_prep_kernel,
        grid=(t // tm,),
        in_specs=[pl.BlockSpec((tm, nq + 2 * nk), lambda i: (i, 0)),
                  pl.BlockSpec((1, ATT_HEAD_DIM), lambda i: (0, 0)),
                  pl.BlockSpec((1, ATT_HEAD_DIM), lambda i: (0, 0)),
                  pl.BlockSpec((tm, ATT_HEAD_DIM), lambda i: (i, 0)),
                  pl.BlockSpec((tm, ATT_HEAD_DIM), lambda i: (i, 0))],
        out_specs=[pl.BlockSpec((tm, nq), lambda i: (i, 0)),
                   pl.BlockSpec((tm, nk), lambda i: (i, 0)),
                   pl.BlockSpec((tm, nk), lambda i: (i, 0))],
        out_shape=[jax.ShapeDtypeStruct((t, nq), BF16),
                   jax.ShapeDtypeStruct((t, nk), BF16),
                   jax.ShapeDtypeStruct((t, nk), BF16)],
        compiler_params=_params("parallel"),
        name="gqa_prep",
    )(qkv, q_gain.reshape(1, -1), k_gain.reshape(1, -1), cos, sin)


def _gqa_mixer(h, w_in_all, layer, q_gain, k_gain, cos, sin, n_lat, n_ctx):
    hd = ATT_HEAD_DIM
    qkv = _mm(h, w_in_all, F32, tn=512, layer=layer)
    gq = q_gain.astype(F32) * (hd ** -0.5 * LOG2E)
    gk = k_gain.astype(F32)
    q, k, v = _gqa_prep(qkv, gq, gk, cos, sin)
    qstat = jnp.full((ATT_HEADS * 8, LANES), hd * NORM_SLACK, F32) * jnp.max(gq * gq)
    kstat = jnp.full((ATT_KV_HEADS * 8, LANES), hd * NORM_SLACK, F32) * jnp.max(gk * gk)
    return _attend(q, [(k, hd, True, 0)], v.T, qstat, kstat, n_heads_kv=ATT_KV_HEADS,
                   groups=ATT_HEADS // ATT_KV_HEADS, dq=hd, dv=hd, n_lat=n_lat, n_ctx=n_ctx)


def _swap_halves_64(x):
    lane = lax.broadcasted_iota(jnp.int32, x.shape, 1)
    return jnp.where(lane % MLA_ROPE < MLA_ROPE // 2, pltpu.roll(x, LANES - MLA_ROPE // 2, 1),
                     pltpu.roll(x, MLA_ROPE // 2, 1))


def _mla_prep_kernel(qn_ref, qr_ref, kr_ref, kn_ref, cos_ref, sin_ref, qf_ref, kk_ref, qstat_ref, kstat_ref, *,
                     qscale):
    cos = cos_ref[...]
    sin = sin_ref[...]
    lane = lax.broadcasted_iota(jnp.int32, cos.shape, 1)
    low = lane < MLA_ROPE
    wq = MLA_NOPE + LANES

    @pl.when(pl.program_id(0) == 0)
    def _():
        qstat_ref[...] = jnp.zeros(qstat_ref.shape, F32)
        kstat_ref[...] = jnp.zeros(kstat_ref.shape, F32)

    kr = jnp.where(low, kr_ref[...], 0.0)
    ky = kr * cos + _swap_halves_64(kr) * sin
    kk = (ky + pltpu.roll(ky, MLA_ROPE, 1)).astype(BF16)
    kk_ref[...] = kk
    for j in range(MLA_HEADS // 2):
        x = qr_ref[:, j * LANES:(j + 1) * LANES]
        y = (x * cos + _swap_halves_64(x) * sin) * qscale
        for h, yr in ((2 * j, jnp.where(low, y, 0.0)), (2 * j + 1, jnp.where(low, 0.0, y))):
            qh = jnp.concatenate([qn_ref[:, h * MLA_NOPE:(h + 1) * MLA_NOPE] * qscale, yr], axis=-1).astype(BF16)
            qf_ref[:, h * wq:(h + 1) * wq] = qh
            _stat_update(qstat_ref, h, qh)
            _stat_update(kstat_ref, h, jnp.concatenate([kn_ref[:, h * MLA_NOPE:(h + 1) * MLA_NOPE], kk], axis=-1))


def _mla_prep(qp, t_in, kvp, cos, sin):
    t = qp.shape[0]
    tm = ROW_TILE
    nn, nr = MLA_HEADS * MLA_NOPE, MLA_HEADS * MLA_ROPE
    qscale = (MLA_NOPE + MLA_ROPE) ** -0.5 * LOG2E
    kr_block = (MLA_Q_RANK + MLA_KV_RANK) // LANES
    return pl.pallas_call(
        functools.partial(_mla_prep_kernel, qscale=qscale),
        grid=(t // tm,),
        in_specs=[pl.BlockSpec((tm, nn), lambda i: (i, 0)),
                  pl.BlockSpec((tm, nr), lambda i: (i, nn // nr)),
                  pl.BlockSpec((tm, LANES), lambda i: (i, kr_block)),
                  pl.BlockSpec((tm, nn), lambda i: (i, 0)),
                  pl.BlockSpec((tm, LANES), lambda i: (i, 0)),
                  pl.BlockSpec((tm, LANES), lambda i: (i, 0))],
        out_specs=[pl.BlockSpec((tm, MLA_HEADS * (MLA_NOPE + LANES)), lambda i: (i, 0)),
                   pl.BlockSpec((tm, LANES), lambda i: (i, 0)),
                   pl.BlockSpec((MLA_HEADS * 8, LANES), lambda i: (0, 0)),
                   pl.BlockSpec((MLA_HEADS * 8, LANES), lambda i: (0, 0))],
        out_shape=[jax.ShapeDtypeStruct((t, MLA_HEADS * (MLA_NOPE + LANES)), BF16),
                   jax.ShapeDtypeStruct((t, LANES), BF16),
                   jax.ShapeDtypeStruct((MLA_HEADS * 8, LANES), F32),
                   jax.ShapeDtypeStruct((MLA_HEADS * 8, LANES), F32)],
        compiler_params=_params("arbitrary"),
        name="mla_prep",
    )(qp, qp, t_in, kvp, cos, sin)


def _mla_mixer(h, w_in, q_gain, kv_gain, w_qb, w_kvb, cos, sin, n_lat, n_ctx):
    pad = -w_in.shape[1] % 512
    t_in = _mm(h, jnp.pad(w_in, ((0, 0), (0, pad))).astype(BF16), F32, tn=512)
    cq = _rms_rows(t_in, q_gain, BF16, width=MLA_Q_RANK, col_block=0)
    ckv = _rms_rows(t_in, kv_gain, BF16, width=MLA_KV_RANK, col_block=MLA_Q_RANK // MLA_KV_RANK)
    wq = w_qb.reshape(MLA_Q_RANK, MLA_HEADS, MLA_NOPE + MLA_ROPE)
    wq = jnp.concatenate([wq[:, :, :MLA_NOPE].reshape(MLA_Q_RANK, -1),
                          wq[:, :, MLA_NOPE:].reshape(MLA_Q_RANK, -1)], axis=1).astype(BF16)
    wkv = w_kvb.reshape(MLA_KV_RANK, MLA_HEADS, MLA_NOPE + MLA_V)
    wk = wkv[:, :, :MLA_NOPE].reshape(MLA_KV_RANK, -1).astype(BF16)
    wv_t = wkv[:, :, MLA_NOPE:].reshape(MLA_KV_RANK, -1).T.astype(BF16)
    qp = _mm(cq, wq, F32, tn=512)
    kvp = _mm(ckv, wk, BF16, tn=512)
    vt = _mm(wv_t, ckv.T, BF16, tn=_pick(ckv.shape[0], (1280, 640, 256)))
    qf, kk, qstat, kstat = _mla_prep(qp, t_in, kvp, cos, sin)
    return _attend(qf, [(kvp, MLA_NOPE, True, 0), (kk, LANES, False, 0)], vt, qstat, kstat,
                   n_heads_kv=MLA_HEADS, groups=1, dq=MLA_NOPE + LANES, dv=MLA_V,
                   n_lat=n_lat, n_ctx=n_ctx)


def _log_sigmoid(x):
    return jnp.minimum(x, 0.0) - jnp.log1p(jnp.exp(-jnp.abs(x)))


def _mlstm_kernel(q_ref, k_ref, v_ref, gc_ref, gr_ref, h_ref, c_sc, m_sc, *, rev):
    L, dk, dvv = ML_L, ML_QK_DIM, ML_V_DIM
    step = pl.program_id(0)

    @pl.when(step == 0)
    def _():
        c_sc[...] = jnp.zeros(c_sc.shape, F32)
        m_sc[...] = jnp.zeros(m_sc.shape, F32)

    t_idx = lax.broadcasted_iota(jnp.int32, (L, L), 0)
    s_idx = lax.broadcasted_iota(jnp.int32, (L, L), 1)
    mask = (s_idx >= t_idx) if rev else (s_idx <= t_idx)
    tri = mask.astype(F32)
    gc = gc_ref[...]
    gr = gr_ref[...]
    b_col = jnp.dot(tri, _log_sigmoid(gc), preferred_element_type=F32, precision=lax.Precision.HIGHEST)
    b_row = lax.dot_general(_log_sigmoid(gr), tri, (((1,), (1,)), ((), ())),
                            preferred_element_type=F32, precision=lax.Precision.HIGHEST)
    last = 0 if rev else L - 1
    i_base, f_base = (2 * ML_HEADS, 3 * ML_HEADS) if rev else (0, ML_HEADS)
    lane_e = lax.broadcasted_iota(jnp.int32, (L, LANES), 1)
    ones_col = jnp.where(lane_e == 0, 1.0, 0.0).astype(BF16)

    for hh in range(ML_HEADS):
        ci, cf = i_base + hh, f_base + hh
        bc = b_col[:, cf:cf + 1]
        br = b_row[cf:cf + 1, :]
        lic = gc[:, ci:ci + 1]
        lir = gr[ci:ci + 1, :]
        g = bc[last:last + 1, :]
        m_prev = m_sc[hh][0:1, 0:1]
        q = q_ref[:, hh * dk:(hh + 1) * dk].astype(BF16)
        kf = k_ref[:, hh * dk:(hh + 1) * dk] * (dk ** -0.5)
        v_ext = jnp.concatenate([v_ref[:, hh * dvv:(hh + 1) * dvv].astype(BF16), ones_col], axis=-1)

        d_log = jnp.where(mask, bc - br + lir, NEG_INF)
        inter = bc + m_prev
        m_t = jnp.maximum(inter, jnp.max(d_log, axis=-1, keepdims=True))
        w_intra = jnp.exp(d_log - m_t)
        w_inter = jnp.exp(inter - m_t)
        s = lax.dot_general(q, kf.astype(BF16), (((1,), (1,)), ((), ())), preferred_element_type=F32) * w_intra
        c_old = c_sc[hh]
        ext = w_inter * jnp.dot(q, c_old.astype(BF16), preferred_element_type=F32) \
            + jnp.dot(s.astype(BF16), v_ext, preferred_element_type=F32)
        den = ext[:, dvv:dvv + 1]
        h_ref[:, hh * dvv:(hh + 1) * dvv] = ext[:, :dvv] / jnp.maximum(jnp.abs(den), jnp.exp(-m_t))

        a = g - bc + lic
        m_new = jnp.maximum(g + m_prev, jnp.max(a, axis=0, keepdims=True))
        w_s = jnp.exp(a - m_new)
        decay = jnp.exp(g + m_prev - m_new)
        kw = (kf * w_s).astype(BF16)
        c_sc[hh] = decay * c_old + lax.dot_general(kw, v_ext, (((0,), (0,)), ((), ())),
                                                   preferred_element_type=F32)
        m_sc[hh] = jnp.broadcast_to(m_new, m_sc.shape[1:])


def _mlstm_scan(qkvo, gates, gates_t, n_lat, n_ctx, rev):
    t = n_lat + n_ctx
    L = ML_L
    nlat, nctx = n_lat // L, n_ctx // L
    nqk, nv = ML_HEADS * ML_QK_DIM, ML_HEADS * ML_V_DIM

    def chunk(c):
        if rev:
            return nlat + nctx - 1 - c
        return jnp.where(c < nctx, nlat + c, c - nctx)

    return pl.pallas_call(
        functools.partial(_mlstm_kernel, rev=rev),
        grid=(nlat + nctx,),
        in_specs=[pl.BlockSpec((L, nqk), lambda c: (chunk(c), 0)),
                  pl.BlockSpec((L, nqk), lambda c: (chunk(c), 1)),
                  pl.BlockSpec((L, nv), lambda c: (chunk(c), (2 * nqk) // nv)),
                  pl.BlockSpec((L, LANES), lambda c: (chunk(c), 0)),
                  pl.BlockSpec((4 * ML_HEADS, L), lambda c: (0, chunk(c)))],
        out_specs=pl.BlockSpec((L, nv), lambda c: (chunk(c), 0)),
        out_shape=jax.ShapeDtypeStruct((t, nv), F32),
        scratch_shapes=[pltpu.VMEM((ML_HEADS, ML_QK_DIM, ML_EXT), F32),
                        pltpu.VMEM((ML_HEADS, 8, LANES), F32)],
        compiler_params=_params("arbitrary"),
        name="mlstm_rev" if rev else "mlstm_fwd",
    )(qkvo, qkvo, qkvo, gates, gates_t)


def _mlstm_finish_kernel(hf_ref, hb_ref, o_ref, g_ref, y_ref):
    dvv = ML_V_DIM
    for hh in range(ML_HEADS):
        sl = slice(hh * dvv, (hh + 1) * dvv)
        h = hf_ref[:, sl] + hb_ref[:, sl]
        hn = h * lax.rsqrt(jnp.mean(h * h, axis=-1, keepdims=True) + EPS) * g_ref[:, sl]
        y_ref[:, sl] = (hn * jax.nn.sigmoid(o_ref[:, sl])).astype(BF16)


def _mlstm_finish(h_f, h_b, qkvo, h_gain):
    t, nv = h_f.shape
    tm = ROW_TILE
    o_block = (2 * ML_HEADS * ML_QK_DIM + nv) // nv
    return pl.pallas_call(
        _mlstm_finish_kernel,
        grid=(t // tm,),
        in_specs=[pl.BlockSpec((tm, nv), lambda i: (i, 0)),
                  pl.BlockSpec((tm, nv), lambda i: (i, 0)),
                  pl.BlockSpec((tm, nv), lambda i: (i, o_block)),
                  pl.BlockSpec((1, nv), lambda i: (0, 0))],
        out_specs=pl.BlockSpec((tm, nv), lambda i: (i, 0)),
        out_shape=jax.ShapeDtypeStruct((t, nv), BF16),
        compiler_params=_params("parallel"),
        name="mlstm_finish",
    )(h_f, h_b, qkvo, h_gain.reshape(1, nv))


def _mlstm_mixer(h, w_in_all, layer, gate_b, h_gain, n_lat, n_ctx):
    n_main = 2 * ML_HEADS * ML_QK_DIM + 2 * ML_HEADS * ML_V_DIM
    n_g = 4 * ML_HEADS
    qkvo = _mm(h, w_in_all, F32, tn=512, n_out=n_main, layer=layer)
    w_g = jnp.pad(w_in_all[layer, :, n_main:], ((0, 0), (0, LANES - n_g))).astype(BF16)
    b_g = jnp.pad(gate_b.astype(F32), (0, LANES - n_g)).reshape(1, LANES)
    gates = _mm(h, w_g, F32, tn=LANES, bias=b_g)
    gates_t = gates[:, :n_g].T
    h_f = _mlstm_scan(qkvo, gates, gates_t, n_lat, n_ctx, False)
    h_b = _mlstm_scan(qkvo, gates, gates_t, n_lat, n_ctx, True)
    return _mlstm_finish(h_f, h_b, qkvo, h_gain)


def _rope_tables(n_lat, n_ctx, rot_dim):
    rows = n_lat // GRID_W
    row = jnp.repeat(jnp.arange(rows, dtype=F32), GRID_W)
    col = jnp.tile(jnp.arange(GRID_W, dtype=F32), rows)
    n_freq = rot_dim // 4
    inv_freq = ROPE_THETA ** (-jnp.arange(n_freq, dtype=F32) / n_freq)
    ang = jnp.concatenate([row[:, None] * inv_freq, col[:, None] * inv_freq], axis=-1)
    cos, sin = jnp.cos(ang), jnp.sin(ang)
    reps = LANES // rot_dim
    cos_t = jnp.tile(jnp.concatenate([cos, cos], axis=-1), (1, reps))
    sin_t = jnp.tile(jnp.concatenate([-sin, sin], axis=-1), (1, reps))
    cos_t = jnp.concatenate([cos_t, jnp.ones((n_ctx, LANES), F32)], axis=0)
    sin_t = jnp.concatenate([sin_t, jnp.zeros((n_ctx, LANES), F32)], axis=0)
    return cos_t, sin_t


def kernel(x, c, ctx, c_ctx, ada_down, ada_up, ada_b, norm_mix, norm_ffn, ffn_in, ffn_out, att_in, att_qnorm,
           att_knorm, att_out, ml_in, ml_gate_b, ml_hnorm, ml_out, mla_in, mla_qnorm, mla_kvnorm, mla_qb,
           mla_kvb, mla_out, final_norm):
    b, n_lat, d = x.shape
    n_ctx = ctx.shape[1]
    depth = ada_down.shape[0]
    assert b == 1 and d == D_MODEL and n_lat % ROW_TILE == 0 and n_ctx % ML_L == 0 and n_lat % ML_L == 0

    cos_a, sin_a = _rope_tables(n_lat, n_ctx, ATT_HEAD_DIM)
    cos_m, sin_m = _rope_tables(n_lat, n_ctx, MLA_ROPE)

    cond = jnp.concatenate([c.reshape(1, d), c_ctx.reshape(1, d), jnp.zeros((6, d), F32)], axis=0)
    mods = _ada_all(cond, ada_down, ada_up, ada_b).reshape(depth, 8, N_MOD, d)

    xs = jnp.concatenate([x[0], ctx[0]], axis=0)
    t = n_lat + n_ctx
    tm_out = _pick(t, (1280, 1024, 640, 512, 256))
    tm_ffn = _pick(t, (640, 512, 256))
    ffn_out_bf16 = ffn_out.astype(BF16)

    for i in range(depth):
        kind, j = i % 3, i // 3
        mod = mods[i, :2]
        h = _rms_rows(xs, norm_mix[i], BF16, mod=mod, ks=0, kc=1, n_lat=n_lat)
        if kind == 0:
            o = _gqa_mixer(h, att_in, j, att_qnorm[j], att_knorm[j], cos_a, sin_a, n_lat, n_ctx)
            w_o = att_out
        elif kind == 1:
            o = _mlstm_mixer(h, ml_in, j, ml_gate_b[j], ml_hnorm[j], n_lat, n_ctx)
            w_o = ml_out
        else:
            o = _mla_mixer(h, mla_in[j], mla_qnorm[j], mla_kvnorm[j], mla_qb[j], mla_kvb[j], cos_m, sin_m,
                           n_lat, n_ctx)
            w_o = mla_out
        xs = _mm_res(o, w_o.astype(BF16), xs, mod[:, 2], n_lat, tm=tm_out, tn=512, tk=w_o.shape[1], layer=j)
        h2 = _rms_rows(xs, norm_ffn[i], BF16, mod=mod, ks=3, kc=4, n_lat=n_lat)
        act = _mm_swiglu(h2, ffn_in, tn=256, layer=i)
        xs = _mm_res(act, ffn_out_bf16, xs, mod[:, 5], n_lat, tm=tm_ffn, tn=1024, tk=D_FF // 2, layer=i)

    out = _rms_rows(xs, final_norm, F32, rows=n_lat)
    return out.reshape(1, n_lat, d)
```

```python
import functools

import jax
import jax.numpy as jnp
from jax import lax
from jax.experimental import pallas as pl
from jax.experimental.pallas import tpu as pltpu

F32 = jnp.float32
BF16 = jnp.bfloat16

D_MODEL = 4096
GRID_W = 64
EPS = 1e-6
ROPE_THETA = 10000.0
ADA_RANK = 256
N_MOD = 6
D_FF = -(-8 * D_MODEL // (3 * 256)) * 256
ATT_HEAD_DIM = 128
ATT_HEADS = D_MODEL // ATT_HEAD_DIM
ATT_KV_HEADS = ATT_HEADS // 4
ML_HEADS = 8
ML_QK_DIM = D_MODEL // 2 // ML_HEADS
ML_V_DIM = D_MODEL // ML_HEADS
MLA_HEADS = D_MODEL // 128
MLA_Q_RANK = D_MODEL // 4
MLA_KV_RANK = 512
MLA_NOPE = 128
MLA_ROPE = 64
MLA_V = 128

LANES = 128
VMEM_LIMIT = 56 * 1024 * 1024
LOG2E = 1.4426950408889634
NEG_INF = float("-inf")
FIXED_MARGIN = 1.02
FIXED_LIMIT = 60.0
NORM_SLACK = 1.02
ROW_TILE = 256
MLA_PAIR = 2
NORM_ROWS = 16
ML_L = 256
ML_EXT = ML_V_DIM + LANES


def _params(*sem):
    return pltpu.CompilerParams(dimension_semantics=sem, vmem_limit_bytes=VMEM_LIMIT)


def _pick(n, cands):
    for c in cands:
        if n % c == 0:
            return c
    raise ValueError(f"no tile for {n} in {cands}")


def _ada_kernel(cond_ref, wd_ref, wu_ref, b_ref, o_ref):
    c = cond_ref[...]
    a = (c * jax.nn.sigmoid(c)).astype(BF16)
    t = jnp.dot(a, wd_ref[...].astype(BF16), preferred_element_type=F32)
    o_ref[...] = jnp.dot(t.astype(BF16), wu_ref[...].astype(BF16), preferred_element_type=F32) + b_ref[...]


def _ada_all(cond, ada_down, ada_up, ada_b):
    depth, d, r = ada_down.shape
    n6 = ada_up.shape[-1]
    tn = 2048
    return pl.pallas_call(
        _ada_kernel,
        grid=(depth, n6 // tn),
        in_specs=[
            pl.BlockSpec((8, d), lambda l, j: (0, 0)),
            pl.BlockSpec((None, d, r), lambda l, j: (l, 0, 0)),
            pl.BlockSpec((None, r, tn), lambda l, j: (l, 0, j)),
            pl.BlockSpec((None, 1, tn), lambda l, j: (l, 0, j)),
        ],
        out_specs=pl.BlockSpec((None, 8, tn), lambda l, j: (l, 0, j)),
        out_shape=jax.ShapeDtypeStruct((depth, 8, n6), F32),
        compiler_params=_params("arbitrary", "arbitrary"),
        name="ada_mod",
    )(cond, ada_down, ada_up, ada_b.reshape(depth, 1, n6))


def _norm_kernel(x_ref, g_ref, *rest, ks, kc):
    o_ref = rest[-1]
    gain = g_ref[...]
    shift = None
    if ks is not None:
        mod_ref = rest[0]
        gain = gain * (1.0 + mod_ref[kc:kc + 1, :])
        shift = mod_ref[ks:ks + 1, :]

    def group(r, carry):
        rows = pl.ds(pl.multiple_of(r * NORM_ROWS, NORM_ROWS), NORM_ROWS)
        x = x_ref[rows, :].astype(F32)
        y = x * lax.rsqrt(jnp.mean(x * x, axis=-1, keepdims=True) + EPS) * gain
        if shift is not None:
            y = y + shift
        o_ref[rows, :] = y.astype(o_ref.dtype)
        return carry

    lax.fori_loop(0, x_ref.shape[0] // NORM_ROWS, group, 0, unroll=2)


def _rms_rows(x, gain, out_dtype, *, width=None, col_block=0, rows=None, mod=None, ks=None, kc=None,
              n_lat=None):
    t = x.shape[0] if rows is None else rows
    width = x.shape[1] if width is None else width
    tm = ROW_TILE
    in_specs = [pl.BlockSpec((tm, width), lambda i: (i, col_block)),
                pl.BlockSpec((1, width), lambda i: (0, 0))]
    args = [x, gain.reshape(1, width).astype(F32)]
    if mod is not None:
        nb_lat = n_lat // tm
        in_specs.append(pl.BlockSpec((None, N_MOD, width), lambda i: (jnp.where(i >= nb_lat, 1, 0), 0, 0)))
        args.append(mod)
    return pl.pallas_call(
        functools.partial(_norm_kernel, ks=ks, kc=kc),
        grid=(t // tm,),
        in_specs=in_specs,
        out_specs=pl.BlockSpec((tm, width), lambda i: (i, 0)),
        out_shape=jax.ShapeDtypeStruct((t, width), out_dtype),
        compiler_params=_params("parallel"),
        name="rms_rows",
    )(*args)


def _mm_kernel(a_ref, w_ref, *rest, has_bias):
    o_ref = rest[-1]
    acc = jnp.dot(a_ref[...], w_ref[...].astype(BF16), preferred_element_type=F32)
    if has_bias:
        acc = acc + rest[0][...]
    o_ref[...] = acc.astype(o_ref.dtype)


def _wspec(w, layer, block, index):
    if layer is None:
        return pl.BlockSpec(block, index)
    return pl.BlockSpec((None,) + block, lambda *g: (layer,) + index(*g))


def _mm(a, w, out_dtype, *, tn, bias=None, n_out=None, layer=None):
    m, k = a.shape
    n = w.shape[-1] if n_out is None else n_out
    tm = _pick(m, (1280, 1024, 640, 512, 256))
    in_specs = [pl.BlockSpec((tm, k), lambda i, j: (i, 0)),
                _wspec(w, layer, (k, tn), lambda i, j: (0, j))]
    args = [a, w]
    if bias is not None:
        in_specs.append(pl.BlockSpec((1, tn), lambda i, j: (0, j)))
        args.append(bias)
    return pl.pallas_call(
        functools.partial(_mm_kernel, has_bias=bias is not None),
        grid=(m // tm, n // tn),
        in_specs=in_specs,
        out_specs=pl.BlockSpec((tm, tn), lambda i, j: (i, j)),
        out_shape=jax.ShapeDtypeStruct((m, n), out_dtype),
        compiler_params=_params("parallel", "arbitrary"),
        name="mm",
    )(*args)


def _mm_res_kernel(a_ref, w_ref, res_ref, gate_ref, o_ref, *scratch, nk, tm, n_lat):
    part = jnp.dot(a_ref[...], w_ref[...].astype(BF16), preferred_element_type=F32)

    def finish(acc):
        row = pl.program_id(0) * tm + lax.broadcasted_iota(jnp.int32, (tm, 1), 0)
        gate = jnp.where(row >= n_lat, gate_ref[1:2, :], gate_ref[0:1, :])
        o_ref[...] = res_ref[...] + gate * acc

    if nk == 1:
        finish(part)
    else:
        acc_ref = scratch[0]
        kk = pl.program_id(2)

        @pl.when(kk == 0)
        def _():
            acc_ref[...] = part

        @pl.when(jnp.logical_and(kk > 0, kk < nk - 1))
        def _():
            acc_ref[...] += part

        @pl.when(kk == nk - 1)
        def _():
            finish(acc_ref[...] + part)


def _mm_res(a, w, res, gate, n_lat, *, tm, tn, tk, layer=None):
    m, k = a.shape
    n = w.shape[-1]
    nk = k // tk
    scratch = [pltpu.VMEM((tm, tn), F32)] if nk > 1 else []
    return pl.pallas_call(
        functools.partial(_mm_res_kernel, nk=nk, tm=tm, n_lat=n_lat),
        grid=(m // tm, n // tn, nk),
        in_specs=[pl.BlockSpec((tm, tk), lambda i, j, kk: (i, kk)),
                  _wspec(w, layer, (tk, tn), lambda i, j, kk: (kk, j)),
                  pl.BlockSpec((tm, tn), lambda i, j, kk: (i, j)),
                  pl.BlockSpec((2, tn), lambda i, j, kk: (0, j))],
        out_specs=pl.BlockSpec((tm, tn), lambda i, j, kk: (i, j)),
        out_shape=jax.ShapeDtypeStruct((m, n), F32),
        scratch_shapes=scratch,
        compiler_params=_params("parallel", "arbitrary", "arbitrary"),
        name="mm_res",
    )(a, w, res, gate)


def _mm_swiglu_kernel(a_ref, wa_ref, wb_ref, o_ref):
    a = a_ref[...]
    u = jnp.dot(a, wa_ref[...].astype(BF16), preferred_element_type=F32)
    v = jnp.dot(a, wb_ref[...].astype(BF16), preferred_element_type=F32)
    o_ref[...] = (u * jax.nn.sigmoid(u) * v).astype(o_ref.dtype)


def _mm_swiglu(a, w, *, tn, layer=None):
    m, k = a.shape
    f = w.shape[-1] // 2
    nb = f // tn
    tm = _pick(m, (1280, 1024, 640, 512, 256))
    return pl.pallas_call(
        _mm_swiglu_kernel,
        grid=(m // tm, nb),
        in_specs=[pl.BlockSpec((tm, k), lambda i, j: (i, 0)),
                  _wspec(w, layer, (k, tn), lambda i, j: (0, j)),
                  _wspec(w, layer, (k, tn), lambda i, j: (0, j + nb))],
        out_specs=pl.BlockSpec((tm, tn), lambda i, j: (i, j)),
        out_shape=jax.ShapeDtypeStruct((m, f), BF16),
        compiler_params=_params("parallel", "arbitrary"),
        name="mm_swiglu",
    )(a, w, w)


def _stat_update(stat_ref, head, y_bf16):
    yf = y_bf16.astype(F32)
    n2 = jnp.max(jnp.sum(yf * yf, axis=-1, keepdims=True), axis=0, keepdims=True)
    sl = slice(head * 8, (head + 1) * 8)
    stat_ref[sl, :] = jnp.maximum(stat_ref[sl, :], jnp.broadcast_to(n2, (8, LANES)))


def _flash_kernel(q_ref, *rest, groups, dq, dv, nkv, n_kparts, has_into, shared_kv):
    k_refs = rest[:n_kparts]
    vt_ref, qstat_ref, kstat_ref = rest[n_kparts:n_kparts + 3]
    o_ref, m_sc, l_sc, acc_sc, fixed_sc = rest[n_kparts + 3 + int(has_into):]
    h = pl.program_id(0)
    kv = pl.program_id(2)
    nt = (((1,), (1,)), ((), ()))

    @pl.when(kv == 0)
    def _():
        bound2 = jnp.max(qstat_ref[...]) * jnp.max(kstat_ref[...]) * (FIXED_MARGIN * FIXED_MARGIN)
        use_fixed = bound2 <= FIXED_LIMIT * FIXED_LIMIT
        fixed_sc[0] = use_fixed.astype(jnp.int32)
        acc_sc[...] = jnp.zeros(acc_sc.shape, F32)
        l_sc[...] = jnp.zeros(l_sc.shape, F32)
        for g in range(groups):
            qf = q_ref[:, g * dq:(g + 1) * dq].astype(F32)
            qn2 = lax.dot_general(jnp.ones((8, dq), BF16), (qf * qf).astype(BF16), nt,
                                  preferred_element_type=F32)[0:1, :]
            kv_head = h if shared_kv else h * groups + g
            kmax2 = kstat_ref[pl.ds(pl.multiple_of(kv_head * 8, 8), 8), :][0:1, 0:1]
            m_fix = jnp.sqrt(qn2 * kmax2) * FIXED_MARGIN
            m_sc[g] = jnp.where(use_fixed, m_fix, NEG_INF)

    fixed = fixed_sc[0] == 1
    running = jnp.logical_not(fixed)

    def load_kv(g):
        if shared_kv:
            parts = [r[...] for r in k_refs]
            vt = vt_ref[...]
        else:
            wk = k_refs[0].shape[1] // groups
            parts = [k_refs[0][:, g * wk:(g + 1) * wk]] + [r[...] for r in k_refs[1:]]
            vt = vt_ref[g * dv:(g + 1) * dv, :]
        return (parts[0] if len(parts) == 1 else jnp.concatenate(parts, axis=-1)), vt

    @pl.when(fixed)
    def _():
        for g in range(groups):
            k, vt = load_kv(g)
            s_t = lax.dot_general(k, q_ref[:, g * dq:(g + 1) * dq], nt, preferred_element_type=F32)
            p_t = jnp.exp2(s_t - m_sc[g])
            l_sc[g] += jnp.sum(p_t, axis=0, keepdims=True)
            acc_sc[g] += jnp.dot(vt, p_t.astype(BF16), preferred_element_type=F32)

    @pl.when(running)
    def _():
        for g in range(groups):
            k, vt = load_kv(g)
            s_t = lax.dot_general(k, q_ref[:, g * dq:(g + 1) * dq], nt, preferred_element_type=F32)
            m_prev = m_sc[g]
            m_new = jnp.maximum(m_prev, jnp.max(s_t, axis=0, keepdims=True))
            alpha = jnp.exp2(m_prev - m_new)
            p_t = jnp.exp2(s_t - m_new)
            l_sc[g] = alpha * l_sc[g] + jnp.sum(p_t, axis=0, keepdims=True)
            acc_sc[g] = alpha * acc_sc[g] + jnp.dot(vt, p_t.astype(BF16), preferred_element_type=F32)
            m_sc[g] = m_new

    @pl.when(kv == nkv - 1)
    def _():
        for g in range(groups):
            o_ref[:, g * dv:(g + 1) * dv] = (acc_sc[g] / l_sc[g]).T.astype(o_ref.dtype)


def _flash(q, vt, qstat, kstat, *, n_heads_kv, groups, dq, dv, q_row0, nq, kv_row0, nkv_rows, tq, tk, k_cols,
           shared_kv, into=None):
    nkv = nkv_rows // tk
    qb0, kb0 = q_row0 // tq, kv_row0 // tk
    in_specs = [pl.BlockSpec((tq, groups * dq), lambda h, i, j: (qb0 + i, h))]
    args = [q]
    for (arr, width, per_head, col0) in k_cols:
        if per_head and not shared_kv:
            in_specs.append(pl.BlockSpec((tk, width * groups), lambda h, i, j, c=col0 // groups: (kb0 + j, c + h)))
        elif per_head:
            in_specs.append(pl.BlockSpec((tk, width), lambda h, i, j, c=col0: (kb0 + j, c + h)))
        else:
            in_specs.append(pl.BlockSpec((tk, width), lambda h, i, j, c=col0: (kb0 + j, c)))
        args.append(arr)
    in_specs.append(pl.BlockSpec((dv if shared_kv else dv * groups, tk), lambda h, i, j: (h, kb0 + j)))
    in_specs.append(pl.BlockSpec(qstat.shape, lambda h, i, j: (0, 0)))
    in_specs.append(pl.BlockSpec(kstat.shape, lambda h, i, j: (0, 0)))
    args += [vt, qstat, kstat]
    aliases = {}
    if into is not None:
        in_specs.append(pl.BlockSpec(memory_space=pl.ANY))
        aliases = {len(args): 0}
        args.append(into)
    return pl.pallas_call(
        functools.partial(_flash_kernel, groups=groups, dq=dq, dv=dv, nkv=nkv, n_kparts=len(k_cols),
                          has_into=into is not None, shared_kv=shared_kv),
        grid=(n_heads_kv, nq // tq, nkv),
        in_specs=in_specs,
        out_specs=pl.BlockSpec((tq, groups * dv), lambda h, i, j: (qb0 + i, h)),
        out_shape=jax.ShapeDtypeStruct((q.shape[0], n_heads_kv * groups * dv), BF16),
        scratch_shapes=[pltpu.VMEM((groups, 1, tq), F32),
                        pltpu.VMEM((groups, 1, tq), F32),
                        pltpu.VMEM((groups, dv, tq), F32),
                        pltpu.SMEM((1,), jnp.int32)],
        input_output_aliases=aliases,
        compiler_params=_params("parallel", "parallel", "arbitrary"),
        name="flash",
    )(*args)


def _attend(q, k_cols, vt, qstat, kstat, *, n_heads_kv, groups, dq, dv, n_lat, n_ctx, shared_kv):
    t = n_lat + n_ctx
    tq = _pick(n_lat, (1024, 512, 256))
    tk = _pick(t, (3328, 1280, 640, 256))
    common = dict(n_heads_kv=n_heads_kv, groups=groups, dq=dq, dv=dv, k_cols=k_cols, shared_kv=shared_kv)
    o = _flash(q, vt, qstat, kstat, q_row0=0, nq=n_lat, kv_row0=0, nkv_rows=t, tq=tq, tk=tk, **common)
    return _flash(q, vt, qstat, kstat, q_row0=n_lat, nq=n_ctx, kv_row0=n_lat, nkv_rows=n_ctx, tq=n_ctx,
                  tk=n_ctx, into=o, **common)


def _gqa_prep_kernel(qkv_ref, qg_ref, kg_ref, cos_ref, sin_ref, q_ref, k_ref, v_ref):
    cos = cos_ref[...]
    sin = sin_ref[...]
    hd = ATT_HEAD_DIM

    def head(col, gain):
        x = qkv_ref[:, col:col + hd]
        y = x * lax.rsqrt(jnp.mean(x * x, axis=-1, keepdims=True) + EPS) * gain
        return (y * cos + pltpu.roll(y, hd // 2, 1) * sin).astype(BF16)

    for h in range(ATT_HEADS):
        q_ref[:, h * hd:(h + 1) * hd] = head(h * hd, qg_ref[...])
    k0 = ATT_HEADS * hd
    for h in range(ATT_KV_HEADS):
        k_ref[:, h * hd:(h + 1) * hd] = head(k0 + h * hd, kg_ref[...])
    v0 = k0 + ATT_KV_HEADS * hd
    v_ref[...] = qkv_ref[:, v0:v0 + ATT_KV_HEADS * hd].astype(BF16)


def _gqa_prep(qkv, q_gain, k_gain, cos, sin):
    t = qkv.shape[0]
    tm = ROW_TILE
    nq, nk = ATT_HEADS * ATT_HEAD_DIM, ATT_KV_HEADS * ATT_HEAD_DIM
    return pl.pallas_call(
        _gqa_prep_kernel,
        grid=(t // tm,),
        in_specs=[pl.BlockSpec((tm, nq + 2 * nk), lambda i: (i, 0)),
                  pl.BlockSpec((1, ATT_HEAD_DIM), lambda i: (0, 0)),
                  pl.BlockSpec((1, ATT_HEAD_DIM), lambda i: (0, 0)),
                  pl.BlockSpec((tm, ATT_HEAD_DIM), lambda i: (i, 0)),
                  pl.BlockSpec((tm, ATT_HEAD_DIM), lambda i: (i, 0))],
        out_specs=[pl.BlockSpec((tm, nq), lambda i: (i, 0)),
                   pl.BlockSpec((tm, nk), lambda i: (i, 0)),
                   pl.BlockSpec((tm, nk), lambda i: (i, 0))],
        out_shape=[jax.ShapeDtypeStruct((t, nq), BF16),
                   jax.ShapeDtypeStruct((t, nk), BF16),
                   jax.ShapeDtypeStruct((t, nk), BF16)],
        compiler_params=_params("parallel"),
        name="gqa_prep",
    )(qkv, q_gain.reshape(1, -1), k_gain.reshape(1, -1), cos, sin)


def _gqa_mixer(h, w_in_all, layer, q_gain, k_gain, cos, sin, n_lat, n_ctx):
    hd = ATT_HEAD_DIM
    qkv = _mm(h, w_in_all, F32, tn=512, layer=layer)
    gq = q_gain.astype(F32) * (hd ** -0.5 * LOG2E)
    gk = k_gain.astype(F32)
    q, k, v = _gqa_prep(qkv, gq, gk, cos, sin)
    qstat = jnp.full((ATT_HEADS * 8, LANES), hd * NORM_SLACK, F32) * jnp.max(gq * gq)
    kstat = jnp.full((ATT_KV_HEADS * 8, LANES), hd * NORM_SLACK, F32) * jnp.max(gk * gk)
    return _attend(q, [(k, hd, True, 0)], v.T, qstat, kstat, n_heads_kv=ATT_KV_HEADS,
                   groups=ATT_HEADS // ATT_KV_HEADS, dq=hd, dv=hd, n_lat=n_lat, n_ctx=n_ctx, shared_kv=True)


def _swap_halves_64(x):
    lane = lax.broadcasted_iota(jnp.int32, x.shape, 1)
    return jnp.where(lane % MLA_ROPE < MLA_ROPE // 2, pltpu.roll(x, LANES - MLA_ROPE // 2, 1),
                     pltpu.roll(x, MLA_ROPE // 2, 1))


def _mla_prep_kernel(qn_ref, qr_ref, kr_ref, kn_ref, cos_ref, sin_ref, qf_ref, kk_ref, qstat_ref, kstat_ref, *,
                     qscale):
    cos = cos_ref[...]
    sin = sin_ref[...]
    lane = lax.broadcasted_iota(jnp.int32, cos.shape, 1)
    low = lane < MLA_ROPE
    wq = MLA_NOPE + LANES

    @pl.when(pl.program_id(0) == 0)
    def _():
        qstat_ref[...] = jnp.zeros(qstat_ref.shape, F32)
        kstat_ref[...] = jnp.zeros(kstat_ref.shape, F32)

    kr = jnp.where(low, kr_ref[...], 0.0)
    ky = kr * cos + _swap_halves_64(kr) * sin
    kk = (ky + pltpu.roll(ky, MLA_ROPE, 1)).astype(BF16)
    kk_ref[...] = kk
    for j in range(MLA_HEADS // 2):
        x = qr_ref[:, j * LANES:(j + 1) * LANES]
        y = (x * cos + _swap_halves_64(x) * sin) * qscale
        for h, yr in ((2 * j, jnp.where(low, y, 0.0)), (2 * j + 1, jnp.where(low, 0.0, y))):
            qh = jnp.concatenate([qn_ref[:, h * MLA_NOPE:(h + 1) * MLA_NOPE] * qscale, yr], axis=-1).astype(BF16)
            qf_ref[:, h * wq:(h + 1) * wq] = qh
            _stat_update(qstat_ref, h, qh)
            _stat_update(kstat_ref, h, jnp.concatenate([kn_ref[:, h * MLA_NOPE:(h + 1) * MLA_NOPE], kk], axis=-1))


def _mla_prep(qp, t_in, kvp, cos, sin):
    t = qp.shape[0]
    tm = ROW_TILE
    nn, nr = MLA_HEADS * MLA_NOPE, MLA_HEADS * MLA_ROPE
    qscale = (MLA_NOPE + MLA_ROPE) ** -0.5 * LOG2E
    kr_block = (MLA_Q_RANK + MLA_KV_RANK) // LANES
    return pl.pallas_call(
        functools.partial(_mla_prep_kernel, qscale=qscale),
        grid=(t // tm,),
        in_specs=[pl.BlockSpec((tm, nn), lambda i: (i, 0)),
                  pl.BlockSpec((tm, nr), lambda i: (i, nn // nr)),
                  pl.BlockSpec((tm, LANES), lambda i: (i, kr_block)),
                  pl.BlockSpec((tm, nn), lambda i: (i, 0)),
                  pl.BlockSpec((tm, LANES), lambda i: (i, 0)),
                  pl.BlockSpec((tm, LANES), lambda i: (i, 0))],
        out_specs=[pl.BlockSpec((tm, MLA_HEADS * (MLA_NOPE + LANES)), lambda i: (i, 0)),
                   pl.BlockSpec((tm, LANES), lambda i: (i, 0)),
                   pl.BlockSpec((MLA_HEADS * 8, LANES), lambda i: (0, 0)),
                   pl.BlockSpec((MLA_HEADS * 8, LANES), lambda i: (0, 0))],
        out_shape=[jax.ShapeDtypeStruct((t, MLA_HEADS * (MLA_NOPE + LANES)), BF16),
                   jax.ShapeDtypeStruct((t, LANES), BF16),
                   jax.ShapeDtypeStruct((MLA_HEADS * 8, LANES), F32),
                   jax.ShapeDtypeStruct((MLA_HEADS * 8, LANES), F32)],
        compiler_params=_params("arbitrary"),
        name="mla_prep",
    )(qp, qp, t_in, kvp, cos, sin)


def _mla_mixer(h, w_in, q_gain, kv_gain, w_qb, w_kvb, cos, sin, n_lat, n_ctx):
    pad = -w_in.shape[1] % 512
    t_in = _mm(h, jnp.pad(w_in, ((0, 0), (0, pad))).astype(BF16), F32, tn=512)
    cq = _rms_rows(t_in, q_gain, BF16, width=MLA_Q_RANK, col_block=0)
    ckv = _rms_rows(t_in, kv_gain, BF16, width=MLA_KV_RANK, col_block=MLA_Q_RANK // MLA_KV_RANK)
    wq = w_qb.reshape(MLA_Q_RANK, MLA_HEADS, MLA_NOPE + MLA_ROPE)
    wq = jnp.concatenate([wq[:, :, :MLA_NOPE].reshape(MLA_Q_RANK, -1),
                          wq[:, :, MLA_NOPE:].reshape(MLA_Q_RANK, -1)], axis=1).astype(BF16)
    wkv = w_kvb.reshape(MLA_KV_RANK, MLA_HEADS, MLA_NOPE + MLA_V)
    wk = wkv[:, :, :MLA_NOPE].reshape(MLA_KV_RANK, -1).astype(BF16)
    wv_t = wkv[:, :, MLA_NOPE:].reshape(MLA_KV_RANK, -1).T.astype(BF16)
    qp = _mm(cq, wq, F32, tn=512)
    kvp = _mm(ckv, wk, BF16, tn=512)
    vt = _mm(wv_t, ckv.T, BF16, tn=_pick(ckv.shape[0], (1280, 640, 256)))
    qf, kk, qstat, kstat = _mla_prep(qp, t_in, kvp, cos, sin)
    return _attend(qf, [(kvp, MLA_NOPE, True, 0), (kk, LANES, False, 0)], vt, qstat, kstat,
                   n_heads_kv=MLA_HEADS // MLA_PAIR, groups=MLA_PAIR, dq=MLA_NOPE + LANES, dv=MLA_V,
                   n_lat=n_lat, n_ctx=n_ctx, shared_kv=False)


def _log_sigmoid(x):
    return jnp.minimum(x, 0.0) - jnp.log1p(jnp.exp(-jnp.abs(x)))


def _mlstm_kernel(q_ref, k_ref, v_ref, gc_ref, gr_ref, h_ref, c_sc, m_sc, *, rev):
    L, dk, dvv = ML_L, ML_QK_DIM, ML_V_DIM
    step = pl.program_id(0)

    @pl.when(step == 0)
    def _():
        c_sc[...] = jnp.zeros(c_sc.shape, F32)
        m_sc[...] = jnp.zeros(m_sc.shape, F32)

    t_idx = lax.broadcasted_iota(jnp.int32, (L, L), 0)
    s_idx = lax.broadcasted_iota(jnp.int32, (L, L), 1)
    mask = (s_idx >= t_idx) if rev else (s_idx <= t_idx)
    tri = mask.astype(F32)
    gc = gc_ref[...]
    gr = gr_ref[...]
    b_col = jnp.dot(tri, _log_sigmoid(gc), preferred_element_type=F32, precision=lax.Precision.HIGHEST)
    b_row = lax.dot_general(_log_sigmoid(gr), tri, (((1,), (1,)), ((), ())),
                            preferred_element_type=F32, precision=lax.Precision.HIGHEST)
    last = 0 if rev else L - 1
    i_base, f_base = (2 * ML_HEADS, 3 * ML_HEADS) if rev else (0, ML_HEADS)
    lane_e = lax.broadcasted_iota(jnp.int32, (L, LANES), 1)
    ones_col = jnp.where(lane_e == 0, 1.0, 0.0).astype(BF16)

    for hh in range(ML_HEADS):
        ci, cf = i_base + hh, f_base + hh
        bc = b_col[:, cf:cf + 1]
        br = b_row[cf:cf + 1, :]
        lic = gc[:, ci:ci + 1]
        lir = gr[ci:ci + 1, :]
        g = bc[last:last + 1, :]
        m_prev = m_sc[hh][0:1, 0:1]
        q = q_ref[:, hh * dk:(hh + 1) * dk].astype(BF16)
        kf = k_ref[:, hh * dk:(hh + 1) * dk] * (dk ** -0.5)
        v_ext = jnp.concatenate([v_ref[:, hh * dvv:(hh + 1) * dvv].astype(BF16), ones_col], axis=-1)

        d_log = jnp.where(mask, bc - br + lir, NEG_INF)
        inter = bc + m_prev
        m_t = jnp.maximum(inter, jnp.max(d_log, axis=-1, keepdims=True))
        w_intra = jnp.exp(d_log - m_t)
        w_inter = jnp.exp(inter - m_t)
        s = lax.dot_general(q, kf.astype(BF16), (((1,), (1,)), ((), ())), preferred_element_type=F32) * w_intra
        c_old = c_sc[hh]
        ext = w_inter * jnp.dot(q, c_old.astype(BF16), preferred_element_type=F32) \
            + jnp.dot(s.astype(BF16), v_ext, preferred_element_type=F32)
        den = ext[:, dvv:dvv + 1]
        h_ref[:, hh * dvv:(hh + 1) * dvv] = ext[:, :dvv] / jnp.maximum(jnp.abs(den), jnp.exp(-m_t))

        a = g - bc + lic
        m_new = jnp.maximum(g + m_prev, jnp.max(a, axis=0, keepdims=True))
        w_s = jnp.exp(a - m_new)
        decay = jnp.exp(g + m_prev - m_new)
        kw = (kf * w_s).astype(BF16)
        c_sc[hh] = decay * c_old + lax.dot_general(kw, v_ext, (((0,), (0,)), ((), ())),
                                                   preferred_element_type=F32)
        m_sc[hh] = jnp.broadcast_to(m_new, m_sc.shape[1:])


def _mlstm_scan(qkvo, gates, gates_t, n_lat, n_ctx, rev):
    t = n_lat + n_ctx
    L = ML_L
    nlat, nctx = n_lat // L, n_ctx // L
    nqk, nv = ML_HEADS * ML_QK_DIM, ML_HEADS * ML_V_DIM

    def chunk(c):
        if rev:
            return nlat + nctx - 1 - c
        return jnp.where(c < nctx, nlat + c, c - nctx)

    return pl.pallas_call(
        functools.partial(_mlstm_kernel, rev=rev),
        grid=(nlat + nctx,),
        in_specs=[pl.BlockSpec((L, nqk), lambda c: (chunk(c), 0)),
                  pl.BlockSpec((L, nqk), lambda c: (chunk(c), 1)),
                  pl.BlockSpec((L, nv), lambda c: (chunk(c), (2 * nqk) // nv)),
                  pl.BlockSpec((L, LANES), lambda c: (chunk(c), 0)),
                  pl.BlockSpec((4 * ML_HEADS, L), lambda c: (0, chunk(c)))],
        out_specs=pl.BlockSpec((L, nv), lambda c: (chunk(c), 0)),
        out_shape=jax.ShapeDtypeStruct((t, nv), F32),
        scratch_shapes=[pltpu.VMEM((ML_HEADS, ML_QK_DIM, ML_EXT), F32),
                        pltpu.VMEM((ML_HEADS, 8, LANES), F32)],
        compiler_params=_params("arbitrary"),
        name="mlstm_rev" if rev else "mlstm_fwd",
    )(qkvo, qkvo, qkvo, gates, gates_t)


def _mlstm_finish_kernel(hf_ref, hb_ref, o_ref, g_ref, y_ref):
    dvv = ML_V_DIM
    for hh in range(ML_HEADS):
        sl = slice(hh * dvv, (hh + 1) * dvv)
        h = hf_ref[:, sl] + hb_ref[:, sl]
        hn = h * lax.rsqrt(jnp.mean(h * h, axis=-1, keepdims=True) + EPS) * g_ref[:, sl]
        y_ref[:, sl] = (hn * jax.nn.sigmoid(o_ref[:, sl])).astype(BF16)


def _mlstm_finish(h_f, h_b, qkvo, h_gain):
    t, nv = h_f.shape
    tm = ROW_TILE
    o_block = (2 * ML_HEADS * ML_QK_DIM + nv) // nv
    return pl.pallas_call(
        _mlstm_finish_kernel,
        grid=(t // tm,),
        in_specs=[pl.BlockSpec((tm, nv), lambda i: (i, 0)),
                  pl.BlockSpec((tm, nv), lambda i: (i, 0)),
                  pl.BlockSpec((tm, nv), lambda i: (i, o_block)),
                  pl.BlockSpec((1, nv), lambda i: (0, 0))],
        out_specs=pl.BlockSpec((tm, nv), lambda i: (i, 0)),
        out_shape=jax.ShapeDtypeStruct((t, nv), BF16),
        compiler_params=_params("parallel"),
        name="mlstm_finish",
    )(h_f, h_b, qkvo, h_gain.reshape(1, nv))


def _mlstm_mixer(h, w_in_all, layer, gate_b, h_gain, n_lat, n_ctx):
    n_main = 2 * ML_HEADS * ML_QK_DIM + 2 * ML_HEADS * ML_V_DIM
    n_g = 4 * ML_HEADS
    qkvo = _mm(h, w_in_all, F32, tn=512, n_out=n_main, layer=layer)
    w_g = jnp.pad(w_in_all[layer, :, n_main:], ((0, 0), (0, LANES - n_g))).astype(BF16)
    b_g = jnp.pad(gate_b.astype(F32), (0, LANES - n_g)).reshape(1, LANES)
    gates = _mm(h, w_g, F32, tn=LANES, bias=b_g)
    gates_t = gates[:, :n_g].T
    h_f = _mlstm_scan(qkvo, gates, gates_t, n_lat, n_ctx, False)
    h_b = _mlstm_scan(qkvo, gates, gates_t, n_lat, n_ctx, True)
    return _mlstm_finish(h_f, h_b, qkvo, h_gain)


def _rope_tables(n_lat, n_ctx, rot_dim):
    rows = n_lat // GRID_W
    row = jnp.repeat(jnp.arange(rows, dtype=F32), GRID_W)
    col = jnp.tile(jnp.arange(GRID_W, dtype=F32), rows)
    n_freq = rot_dim // 4
    inv_freq = ROPE_THETA ** (-jnp.arange(n_freq, dtype=F32) / n_freq)
    ang = jnp.concatenate([row[:, None] * inv_freq, col[:, None] * inv_freq], axis=-1)
    cos, sin = jnp.cos(ang), jnp.sin(ang)
    reps = LANES // rot_dim
    cos_t = jnp.tile(jnp.concatenate([cos, cos], axis=-1), (1, reps))
    sin_t = jnp.tile(jnp.concatenate([-sin, sin], axis=-1), (1, reps))
    cos_t = jnp.concatenate([cos_t, jnp.ones((n_ctx, LANES), F32)], axis=0)
    sin_t = jnp.concatenate([sin_t, jnp.zeros((n_ctx, LANES), F32)], axis=0)
    return cos_t, sin_t


def kernel(x, c, ctx, c_ctx, ada_down, ada_up, ada_b, norm_mix, norm_ffn, ffn_in, ffn_out, att_in, att_qnorm,
           att_knorm, att_out, ml_in, ml_gate_b, ml_hnorm, ml_out, mla_in, mla_qnorm, mla_kvnorm, mla_qb,
           mla_kvb, mla_out, final_norm):
    b, n_lat, d = x.shape
    n_ctx = ctx.shape[1]
    depth = ada_down.shape[0]
    assert b == 1 and d == D_MODEL and n_lat % ROW_TILE == 0 and n_ctx % ML_L == 0 and n_lat % ML_L == 0

    cos_a, sin_a = _rope_tables(n_lat, n_ctx, ATT_HEAD_DIM)
    cos_m, sin_m = _rope_tables(n_lat, n_ctx, MLA_ROPE)

    cond = jnp.concatenate([c.reshape(1, d), c_ctx.reshape(1, d), jnp.zeros((6, d), F32)], axis=0)
    mods = _ada_all(cond, ada_down, ada_up, ada_b).reshape(depth, 8, N_MOD, d)

    xs = jnp.concatenate([x[0], ctx[0]], axis=0)
    t = n_lat + n_ctx
    tm_out = _pick(t, (1280, 1024, 640, 512, 256))
    tm_ffn = _pick(t, (640, 512, 256))
    ffn_out_bf16 = ffn_out.astype(BF16)

    for i in range(depth):
        kind, j = i % 3, i // 3
        mod = mods[i, :2]
        h = _rms_rows(xs, norm_mix[i], BF16, mod=mod, ks=0, kc=1, n_lat=n_lat)
        if kind == 0:
            o = _gqa_mixer(h, att_in, j, att_qnorm[j], att_knorm[j], cos_a, sin_a, n_lat, n_ctx)
            w_o = att_out
        elif kind == 1:
            o = _mlstm_mixer(h, ml_in, j, ml_gate_b[j], ml_hnorm[j], n_lat, n_ctx)
            w_o = ml_out
        else:
            o = _mla_mixer(h, mla_in[j], mla_qnorm[j], mla_kvnorm[j], mla_qb[j], mla_kvb[j], cos_m, sin_m,
                           n_lat, n_ctx)
            w_o = mla_out
        xs = _mm_res(o, w_o.astype(BF16), xs, mod[:, 2], n_lat, tm=tm_out, tn=512, tk=w_o.shape[1], layer=j)
        h2 = _rms_rows(xs, norm_ffn[i], BF16, mod=mod, ks=3, kc=4, n_lat=n_lat)
        act = _mm_swiglu(h2, ffn_in, tn=256, layer=i)
        xs = _mm_res(act, ffn_out_bf16, xs, mod[:, 5], n_lat, tm=tm_ffn, tn=1024, tk=D_FF // 2, layer=i)

    out = _rms_rows(xs, final_norm, F32, rows=n_lat)
    return out.reshape(1, n_lat, d)
```

```python
import functools

import jax
import jax.numpy as jnp
from jax import lax
from jax.experimental import pallas as pl
from jax.experimental.pallas import tpu as pltpu

F32 = jnp.float32
BF16 = jnp.bfloat16

D_MODEL = 4096
GRID_W = 64
EPS = 1e-6
ROPE_THETA = 10000.0
ADA_RANK = 256
N_MOD = 6
D_FF = -(-8 * D_MODEL // (3 * 256)) * 256
ATT_HEAD_DIM = 128
ATT_HEADS = D_MODEL // ATT_HEAD_DIM
ATT_KV_HEADS = ATT_HEADS // 4
ML_HEADS = 8
ML_QK_DIM = D_MODEL // 2 // ML_HEADS
ML_V_DIM = D_MODEL // ML_HEADS
MLA_HEADS = D_MODEL // 128
MLA_Q_RANK = D_MODEL // 4
MLA_KV_RANK = 512
MLA_NOPE = 128
MLA_ROPE = 64
MLA_V = 128

LANES = 128
VMEM_LIMIT = 56 * 1024 * 1024
LOG2E = 1.4426950408889634
NEG_INF = float("-inf")
FIXED_MARGIN = 1.02
FIXED_LIMIT = 60.0
NORM_SLACK = 1.02
ROW_TILE = 256
MLA_PAIR = 2
NORM_ROWS = 16
ML_L = 256
ML_EXT = ML_V_DIM + LANES


def _params(*sem):
    return pltpu.CompilerParams(dimension_semantics=sem, vmem_limit_bytes=VMEM_LIMIT)


def _pick(n, cands):
    for c in cands:
        if n % c == 0:
            return c
    raise ValueError(f"no tile for {n} in {cands}")


def _ada_kernel(cond_ref, wd_ref, wu_ref, b_ref, o_ref):
    c = cond_ref[...]
    a = (c * jax.nn.sigmoid(c)).astype(BF16)
    t = jnp.dot(a, wd_ref[...].astype(BF16), preferred_element_type=F32)
    o_ref[...] = jnp.dot(t.astype(BF16), wu_ref[...].astype(BF16), preferred_element_type=F32) + b_ref[...]


def _ada_all(cond, ada_down, ada_up, ada_b):
    depth, d, r = ada_down.shape
    n6 = ada_up.shape[-1]
    tn = 2048
    return pl.pallas_call(
        _ada_kernel,
        grid=(depth, n6 // tn),
        in_specs=[
            pl.BlockSpec((8, d), lambda l, j: (0, 0)),
            pl.BlockSpec((None, d, r), lambda l, j: (l, 0, 0)),
            pl.BlockSpec((None, r, tn), lambda l, j: (l, 0, j)),
            pl.BlockSpec((None, 1, tn), lambda l, j: (l, 0, j)),
        ],
        out_specs=pl.BlockSpec((None, 8, tn), lambda l, j: (l, 0, j)),
        out_shape=jax.ShapeDtypeStruct((depth, 8, n6), F32),
        compiler_params=_params("arbitrary", "arbitrary"),
        name="ada_mod",
    )(cond, ada_down, ada_up, ada_b.reshape(depth, 1, n6))


def _norm_kernel(x_ref, g_ref, *rest, ks, kc):
    o_ref = rest[-1]
    gain = g_ref[...]
    shift = None
    if ks is not None:
        mod_ref = rest[0]
        gain = gain * (1.0 + mod_ref[kc:kc + 1, :])
        shift = mod_ref[ks:ks + 1, :]

    def group(r, carry):
        rows = pl.ds(pl.multiple_of(r * NORM_ROWS, NORM_ROWS), NORM_ROWS)
        x = x_ref[rows, :].astype(F32)
        y = x * lax.rsqrt(jnp.mean(x * x, axis=-1, keepdims=True) + EPS) * gain
        if shift is not None:
            y = y + shift
        o_ref[rows, :] = y.astype(o_ref.dtype)
        return carry

    lax.fori_loop(0, x_ref.shape[0] // NORM_ROWS, group, 0, unroll=2)


def _rms_rows(x, gain, out_dtype, *, width=None, col_block=0, rows=None, mod=None, ks=None, kc=None,
              n_lat=None):
    t = x.shape[0] if rows is None else rows
    width = x.shape[1] if width is None else width
    tm = ROW_TILE
    in_specs = [pl.BlockSpec((tm, width), lambda i: (i, col_block)),
                pl.BlockSpec((1, width), lambda i: (0, 0))]
    args = [x, gain.reshape(1, width).astype(F32)]
    if mod is not None:
        nb_lat = n_lat // tm
        in_specs.append(pl.BlockSpec((None, N_MOD, width), lambda i: (jnp.where(i >= nb_lat, 1, 0), 0, 0)))
        args.append(mod)
    return pl.pallas_call(
        functools.partial(_norm_kernel, ks=ks, kc=kc),
        grid=(t // tm,),
        in_specs=in_specs,
        out_specs=pl.BlockSpec((tm, width), lambda i: (i, 0)),
        out_shape=jax.ShapeDtypeStruct((t, width), out_dtype),
        compiler_params=_params("parallel"),
        name="rms_rows",
    )(*args)


def _mm_kernel(a_ref, w_ref, *rest, has_bias):
    o_ref = rest[-1]
    acc = jnp.dot(a_ref[...], w_ref[...].astype(BF16), preferred_element_type=F32)
    if has_bias:
        acc = acc + rest[0][...]
    o_ref[...] = acc.astype(o_ref.dtype)


def _wspec(w, layer, block, index):
    if layer is None:
        return pl.BlockSpec(block, index)
    return pl.BlockSpec((None,) + block, lambda *g: (layer,) + index(*g))


def _mm(a, w, out_dtype, *, tn, bias=None, n_out=None, layer=None):
    m, k = a.shape
    n = w.shape[-1] if n_out is None else n_out
    tm = _pick(m, (1280, 1024, 640, 512, 256))
    in_specs = [pl.BlockSpec((tm, k), lambda i, j: (i, 0)),
                _wspec(w, layer, (k, tn), lambda i, j: (0, j))]
    args = [a, w]
    if bias is not None:
        in_specs.append(pl.BlockSpec((1, tn), lambda i, j: (0, j)))
        args.append(bias)
    return pl.pallas_call(
        functools.partial(_mm_kernel, has_bias=bias is not None),
        grid=(m // tm, n // tn),
        in_specs=in_specs,
        out_specs=pl.BlockSpec((tm, tn), lambda i, j: (i, j)),
        out_shape=jax.ShapeDtypeStruct((m, n), out_dtype),
        compiler_params=_params("parallel", "arbitrary"),
        name="mm",
    )(*args)


def _mm_res_kernel(a_ref, w_ref, res_ref, gate_ref, o_ref, *scratch, nk, tm, n_lat):
    part = jnp.dot(a_ref[...], w_ref[...].astype(BF16), preferred_element_type=F32)

    def finish(acc):
        row = pl.program_id(0) * tm + lax.broadcasted_iota(jnp.int32, (tm, 1), 0)
        gate = jnp.where(row >= n_lat, gate_ref[1:2, :], gate_ref[0:1, :])
        o_ref[...] = res_ref[...] + gate * acc

    if nk == 1:
        finish(part)
    else:
        acc_ref = scratch[0]

        @pl.when(pl.program_id(2) == 0)
        def _():
            acc_ref[...] = jnp.zeros(acc_ref.shape, F32)

        acc = acc_ref[...] + part
        acc_ref[...] = acc
        finish(acc)


def _mm_res(a, w, res, gate, n_lat, *, tm, tn, tk, layer=None):
    m, k = a.shape
    n = w.shape[-1]
    nk = k // tk
    scratch = [pltpu.VMEM((tm, tn), F32)] if nk > 1 else []
    return pl.pallas_call(
        functools.partial(_mm_res_kernel, nk=nk, tm=tm, n_lat=n_lat),
        grid=(m // tm, n // tn, nk),
        in_specs=[pl.BlockSpec((tm, tk), lambda i, j, kk: (i, kk)),
                  _wspec(w, layer, (tk, tn), lambda i, j, kk: (kk, j)),
                  pl.BlockSpec((tm, tn), lambda i, j, kk: (i, j)),
                  pl.BlockSpec((2, tn), lambda i, j, kk: (0, j))],
        out_specs=pl.BlockSpec((tm, tn), lambda i, j, kk: (i, j)),
        out_shape=jax.ShapeDtypeStruct((m, n), F32),
        scratch_shapes=scratch,
        compiler_params=_params("parallel", "arbitrary", "arbitrary"),
        name="mm_res",
    )(a, w, res, gate)


def _mm_swiglu_kernel(a_ref, wa_ref, wb_ref, o_ref):
    a = a_ref[...]
    u = jnp.dot(a, wa_ref[...].astype(BF16), preferred_element_type=F32)
    v = jnp.dot(a, wb_ref[...].astype(BF16), preferred_element_type=F32)
    o_ref[...] = (u * jax.nn.sigmoid(u) * v).astype(o_ref.dtype)


def _mm_swiglu(a, w, *, tn, layer=None):
    m, k = a.shape
    f = w.shape[-1] // 2
    nb = f // tn
    tm = _pick(m, (1280, 1024, 640, 512, 256))
    return pl.pallas_call(
        _mm_swiglu_kernel,
        grid=(m // tm, nb),
        in_specs=[pl.BlockSpec((tm, k), lambda i, j: (i, 0)),
                  _wspec(w, layer, (k, tn), lambda i, j: (0, j)),
                  _wspec(w, layer, (k, tn), lambda i, j: (0, j + nb))],
        out_specs=pl.BlockSpec((tm, tn), lambda i, j: (i, j)),
        out_shape=jax.ShapeDtypeStruct((m, f), BF16),
        compiler_params=_params("parallel", "arbitrary"),
        name="mm_swiglu",
    )(a, w, w)


def _stat_update(stat_ref, head, y_bf16):
    yf = y_bf16.astype(F32)
    n2 = jnp.max(jnp.sum(yf * yf, axis=-1, keepdims=True), axis=0, keepdims=True)
    sl = slice(head * 8, (head + 1) * 8)
    stat_ref[sl, :] = jnp.maximum(stat_ref[sl, :], jnp.broadcast_to(n2, (8, LANES)))


def _flash_kernel(q_ref, *rest, groups, dq, dv, nkv, n_kparts, has_into, shared_kv):
    k_refs = rest[:n_kparts]
    vt_ref, qstat_ref, kstat_ref = rest[n_kparts:n_kparts + 3]
    o_ref, m_sc, l_sc, acc_sc, fixed_sc = rest[n_kparts + 3 + int(has_into):]
    h = pl.program_id(0)
    kv = pl.program_id(2)
    nt = (((1,), (1,)), ((), ()))

    @pl.when(kv == 0)
    def _():
        bound2 = jnp.max(qstat_ref[...]) * jnp.max(kstat_ref[...]) * (FIXED_MARGIN * FIXED_MARGIN)
        use_fixed = bound2 <= FIXED_LIMIT * FIXED_LIMIT
        fixed_sc[0] = use_fixed.astype(jnp.int32)
        acc_sc[...] = jnp.zeros(acc_sc.shape, F32)
        l_sc[...] = jnp.zeros(l_sc.shape, F32)
        for g in range(groups):
            qf = q_ref[:, g * dq:(g + 1) * dq].astype(F32)
            qn2 = lax.dot_general(jnp.ones((8, dq), BF16), (qf * qf).astype(BF16), nt,
                                  preferred_element_type=F32)[0:1, :]
            kv_head = h if shared_kv else h * groups + g
            kmax2 = kstat_ref[pl.ds(pl.multiple_of(kv_head * 8, 8), 8), :][0:1, 0:1]
            m_fix = jnp.sqrt(qn2 * kmax2) * FIXED_MARGIN
            m_sc[g] = jnp.where(use_fixed, m_fix, NEG_INF)

    fixed = fixed_sc[0] == 1
    running = jnp.logical_not(fixed)

    def load_kv(g):
        if shared_kv:
            parts = [r[...] for r in k_refs]
            vt = vt_ref[...]
        else:
            wk = k_refs[0].shape[1] // groups
            parts = [k_refs[0][:, g * wk:(g + 1) * wk]] + [r[...] for r in k_refs[1:]]
            vt = vt_ref[g * dv:(g + 1) * dv, :]
        return (parts[0] if len(parts) == 1 else jnp.concatenate(parts, axis=-1)), vt

    @pl.when(fixed)
    def _():
        for g in range(groups):
            k, vt = load_kv(g)
            s_t = lax.dot_general(k, q_ref[:, g * dq:(g + 1) * dq], nt, preferred_element_type=F32)
            p_t = jnp.exp2(s_t - m_sc[g])
            l_sc[g] += jnp.sum(p_t, axis=0, keepdims=True)
            acc_sc[g] += jnp.dot(vt, p_t.astype(BF16), preferred_element_type=F32)

    @pl.when(running)
    def _():
        for g in range(groups):
            k, vt = load_kv(g)
            s_t = lax.dot_general(k, q_ref[:, g * dq:(g + 1) * dq], nt, preferred_element_type=F32)
            m_prev = m_sc[g]
            m_new = jnp.maximum(m_prev, jnp.max(s_t, axis=0, keepdims=True))
            alpha = jnp.exp2(m_prev - m_new)
            p_t = jnp.exp2(s_t - m_new)
            l_sc[g] = alpha * l_sc[g] + jnp.sum(p_t, axis=0, keepdims=True)
            acc_sc[g] = alpha * acc_sc[g] + jnp.dot(vt, p_t.astype(BF16), preferred_element_type=F32)
            m_sc[g] = m_new

    @pl.when(kv == nkv - 1)
    def _():
        for g in range(groups):
            o_ref[:, g * dv:(g + 1) * dv] = (acc_sc[g] / l_sc[g]).T.astype(o_ref.dtype)


def _flash(q, vt, qstat, kstat, *, n_heads_kv, groups, dq, dv, q_row0, nq, kv_row0, nkv_rows, tq, tk, k_cols,
           shared_kv, into=None):
    nkv = nkv_rows // tk
    qb0, kb0 = q_row0 // tq, kv_row0 // tk
    in_specs = [pl.BlockSpec((tq, groups * dq), lambda h, i, j: (qb0 + i, h))]
    args = [q]
    for (arr, width, per_head, col0) in k_cols:
        if per_head and not shared_kv:
            in_specs.append(pl.BlockSpec((tk, width * groups), lambda h, i, j, c=col0 // groups: (kb0 + j, c + h)))
        elif per_head:
            in_specs.append(pl.BlockSpec((tk, width), lambda h, i, j, c=col0: (kb0 + j, c + h)))
        else:
            in_specs.append(pl.BlockSpec((tk, width), lambda h, i, j, c=col0: (kb0 + j, c)))
        args.append(arr)
    in_specs.append(pl.BlockSpec((dv if shared_kv else dv * groups, tk), lambda h, i, j: (h, kb0 + j)))
    in_specs.append(pl.BlockSpec(qstat.shape, lambda h, i, j: (0, 0)))
    in_specs.append(pl.BlockSpec(kstat.shape, lambda h, i, j: (0, 0)))
    args += [vt, qstat, kstat]
    aliases = {}
    if into is not None:
        in_specs.append(pl.BlockSpec(memory_space=pl.ANY))
        aliases = {len(args): 0}
        args.append(into)
    return pl.pallas_call(
        functools.partial(_flash_kernel, groups=groups, dq=dq, dv=dv, nkv=nkv, n_kparts=len(k_cols),
                          has_into=into is not None, shared_kv=shared_kv),
        grid=(n_heads_kv, nq // tq, nkv),
        in_specs=in_specs,
        out_specs=pl.BlockSpec((tq, groups * dv), lambda h, i, j: (qb0 + i, h)),
        out_shape=jax.ShapeDtypeStruct((q.shape[0], n_heads_kv * groups * dv), BF16),
        scratch_shapes=[pltpu.VMEM((groups, 1, tq), F32),
                        pltpu.VMEM((groups, 1, tq), F32),
                        pltpu.VMEM((groups, dv, tq), F32),
                        pltpu.SMEM((1,), jnp.int32)],
        input_output_aliases=aliases,
        compiler_params=_params("parallel", "parallel", "arbitrary"),
        name="flash",
    )(*args)


def _attend(q, k_cols, vt, qstat, kstat, *, n_heads_kv, groups, dq, dv, n_lat, n_ctx, shared_kv):
    t = n_lat + n_ctx
    tq = _pick(n_lat, (1024, 512, 256))
    tk = _pick(t, (3328, 1280, 640, 256))
    common = dict(n_heads_kv=n_heads_kv, groups=groups, dq=dq, dv=dv, k_cols=k_cols, shared_kv=shared_kv)
    o = _flash(q, vt, qstat, kstat, q_row0=0, nq=n_lat, kv_row0=0, nkv_rows=t, tq=tq, tk=tk, **common)
    return _flash(q, vt, qstat, kstat, q_row0=n_lat, nq=n_ctx, kv_row0=n_lat, nkv_rows=n_ctx, tq=n_ctx,
                  tk=n_ctx, into=o, **common)


def _gqa_prep_kernel(qkv_ref, qg_ref, kg_ref, cos_ref, sin_ref, q_ref, k_ref, v_ref):
    cos = cos_ref[...]
    sin = sin_ref[...]
    hd = ATT_HEAD_DIM

    def head(col, gain):
        x = qkv_ref[:, col:col + hd]
        y = x * lax.rsqrt(jnp.mean(x * x, axis=-1, keepdims=True) + EPS) * gain
        return (y * cos + pltpu.roll(y, hd // 2, 1) * sin).astype(BF16)

    for h in range(ATT_HEADS):
        q_ref[:, h * hd:(h + 1) * hd] = head(h * hd, qg_ref[...])
    k0 = ATT_HEADS * hd
    for h in range(ATT_KV_HEADS):
        k_ref[:, h * hd:(h + 1) * hd] = head(k0 + h * hd, kg_ref[...])
    v0 = k0 + ATT_KV_HEADS * hd
    v_ref[...] = qkv_ref[:, v0:v0 + ATT_KV_HEADS * hd].astype(BF16)


def _gqa_prep(qkv, q_gain, k_gain, cos, sin):
    t = qkv.shape[0]
    tm = ROW_TILE
    nq, nk = ATT_HEADS * ATT_HEAD_DIM, ATT_KV_HEADS * ATT_HEAD_DIM
    return pl.pallas_call(
        _gqa_prep_kernel,
        grid=(t // tm,),
        in_specs=[pl.BlockSpec((tm, nq + 2 * nk), lambda i: (i, 0)),
                  pl.BlockSpec((1, ATT_HEAD_DIM), lambda i: (0, 0)),
                  pl.BlockSpec((1, ATT_HEAD_DIM), lambda i: (0, 0)),
                  pl.BlockSpec((tm, ATT_HEAD_DIM), lambda i: (i, 0)),
                  pl.BlockSpec((tm, ATT_HEAD_DIM), lambda i: (i, 0))],
        out_specs=[pl.BlockSpec((tm, nq), lambda i: (i, 0)),
                   pl.BlockSpec((tm, nk), lambda i: (i, 0)),
                   pl.BlockSpec((tm, nk), lambda i: (i, 0))],
        out_shape=[jax.ShapeDtypeStruct((t, nq), BF16),
                   jax.ShapeDtypeStruct((t, nk), BF16),
                   jax.ShapeDtypeStruct((t, nk), BF16)],
        compiler_params=_params("parallel"),
        name="gqa_prep",
    )(qkv, q_gain.reshape(1, -1), k_gain.reshape(1, -1), cos, sin)


def _gqa_mixer(h, w_in_all, layer, q_gain, k_gain, cos, sin, n_lat, n_ctx):
    hd = ATT_HEAD_DIM
    qkv = _mm(h, w_in_all, F32, tn=512, layer=layer)
    gq = q_gain.astype(F32) * (hd ** -0.5 * LOG2E)
    gk = k_gain.astype(F32)
    q, k, v = _gqa_prep(qkv, gq, gk, cos, sin)
    qstat = jnp.full((ATT_HEADS * 8, LANES), hd * NORM_SLACK, F32) * jnp.max(gq * gq)
    kstat = jnp.full((ATT_KV_HEADS * 8, LANES), hd * NORM_SLACK, F32) * jnp.max(gk * gk)
    return _attend(q, [(k, hd, True, 0)], v.T, qstat, kstat, n_heads_kv=ATT_KV_HEADS,
                   groups=ATT_HEADS // ATT_KV_HEADS, dq=hd, dv=hd, n_lat=n_lat, n_ctx=n_ctx, shared_kv=True)


def _swap_halves_64(x):
    lane = lax.broadcasted_iota(jnp.int32, x.shape, 1)
    return jnp.where(lane % MLA_ROPE < MLA_ROPE // 2, pltpu.roll(x, LANES - MLA_ROPE // 2, 1),
                     pltpu.roll(x, MLA_ROPE // 2, 1))


def _mla_prep_kernel(qn_ref, qr_ref, kr_ref, kn_ref, cos_ref, sin_ref, qf_ref, kk_ref, qstat_ref, kstat_ref, *,
                     qscale):
    cos = cos_ref[...]
    sin = sin_ref[...]
    lane = lax.broadcasted_iota(jnp.int32, cos.shape, 1)
    low = lane < MLA_ROPE
    wq = MLA_NOPE + LANES

    @pl.when(pl.program_id(0) == 0)
    def _():
        qstat_ref[...] = jnp.zeros(qstat_ref.shape, F32)
        kstat_ref[...] = jnp.zeros(kstat_ref.shape, F32)

    kr = jnp.where(low, kr_ref[...], 0.0)
    ky = kr * cos + _swap_halves_64(kr) * sin
    kk = (ky + pltpu.roll(ky, MLA_ROPE, 1)).astype(BF16)
    kk_ref[...] = kk
    for j in range(MLA_HEADS // 2):
        x = qr_ref[:, j * LANES:(j + 1) * LANES]
        y = (x * cos + _swap_halves_64(x) * sin) * qscale
        for h, yr in ((2 * j, jnp.where(low, y, 0.0)), (2 * j + 1, jnp.where(low, 0.0, y))):
            qh = jnp.concatenate([qn_ref[:, h * MLA_NOPE:(h + 1) * MLA_NOPE] * qscale, yr], axis=-1).astype(BF16)
            qf_ref[:, h * wq:(h + 1) * wq] = qh
            _stat_update(qstat_ref, h, qh)
            _stat_update(kstat_ref, h, jnp.concatenate([kn_ref[:, h * MLA_NOPE:(h + 1) * MLA_NOPE], kk], axis=-1))


def _mla_prep(qp, t_in, kvp, cos, sin):
    t = qp.shape[0]
    tm = ROW_TILE
    nn, nr = MLA_HEADS * MLA_NOPE, MLA_HEADS * MLA_ROPE
    qscale = (MLA_NOPE + MLA_ROPE) ** -0.5 * LOG2E
    kr_block = (MLA_Q_RANK + MLA_KV_RANK) // LANES
    return pl.pallas_call(
        functools.partial(_mla_prep_kernel, qscale=qscale),
        grid=(t // tm,),
        in_specs=[pl.BlockSpec((tm, nn), lambda i: (i, 0)),
                  pl.BlockSpec((tm, nr), lambda i: (i, nn // nr)),
                  pl.BlockSpec((tm, LANES), lambda i: (i, kr_block)),
                  pl.BlockSpec((tm, nn), lambda i: (i, 0)),
                  pl.BlockSpec((tm, LANES), lambda i: (i, 0)),
                  pl.BlockSpec((tm, LANES), lambda i: (i, 0))],
        out_specs=[pl.BlockSpec((tm, MLA_HEADS * (MLA_NOPE + LANES)), lambda i: (i, 0)),
                   pl.BlockSpec((tm, LANES), lambda i: (i, 0)),
                   pl.BlockSpec((MLA_HEADS * 8, LANES), lambda i: (0, 0)),
                   pl.BlockSpec((MLA_HEADS * 8, LANES), lambda i: (0, 0))],
        out_shape=[jax.ShapeDtypeStruct((t, MLA_HEADS * (MLA_NOPE + LANES)), BF16),
                   jax.ShapeDtypeStruct((t, LANES), BF16),
                   jax.ShapeDtypeStruct((MLA_HEADS * 8, LANES), F32),
                   jax.ShapeDtypeStruct((MLA_HEADS * 8, LANES), F32)],
        compiler_params=_params("arbitrary"),
        name="mla_prep",
    )(qp, qp, t_in, kvp, cos, sin)


def _mla_mixer(h, w_in, q_gain, kv_gain, w_qb, w_kvb, cos, sin, n_lat, n_ctx):
    pad = -w_in.shape[1] % 512
    t_in = _mm(h, jnp.pad(w_in, ((0, 0), (0, pad))).astype(BF16), F32, tn=512)
    cq = _rms_rows(t_in, q_gain, BF16, width=MLA_Q_RANK, col_block=0)
    ckv = _rms_rows(t_in, kv_gain, BF16, width=MLA_KV_RANK, col_block=MLA_Q_RANK // MLA_KV_RANK)
    wq = w_qb.reshape(MLA_Q_RANK, MLA_HEADS, MLA_NOPE + MLA_ROPE)
    wq = jnp.concatenate([wq[:, :, :MLA_NOPE].reshape(MLA_Q_RANK, -1),
                          wq[:, :, MLA_NOPE:].reshape(MLA_Q_RANK, -1)], axis=1).astype(BF16)
    wkv = w_kvb.reshape(MLA_KV_RANK, MLA_HEADS, MLA_NOPE + MLA_V)
    wk = wkv[:, :, :MLA_NOPE].reshape(MLA_KV_RANK, -1).astype(BF16)
    wv_t = wkv[:, :, MLA_NOPE:].reshape(MLA_KV_RANK, -1).T.astype(BF16)
    qp = _mm(cq, wq, F32, tn=512)
    kvp = _mm(ckv, wk, BF16, tn=512)
    vt = _mm(wv_t, ckv.T, BF16, tn=_pick(ckv.shape[0], (1280, 640, 256)))
    qf, kk, qstat, kstat = _mla_prep(qp, t_in, kvp, cos, sin)
    return _attend(qf, [(kvp, MLA_NOPE, True, 0), (kk, LANES, False, 0)], vt, qstat, kstat,
                   n_heads_kv=MLA_HEADS // MLA_PAIR, groups=MLA_PAIR, dq=MLA_NOPE + LANES, dv=MLA_V,
                   n_lat=n_lat, n_ctx=n_ctx, shared_kv=False)


def _log_sigmoid(x):
    return jnp.minimum(x, 0.0) - jnp.log1p(jnp.exp(-jnp.abs(x)))


def _mlstm_kernel(q_ref, k_ref, v_ref, gc_ref, gr_ref, h_ref, c_sc, m_sc, *, rev):
    L, dk, dvv = ML_L, ML_QK_DIM, ML_V_DIM
    step = pl.program_id(0)

    @pl.when(step == 0)
    def _():
        c_sc[...] = jnp.zeros(c_sc.shape, F32)
        m_sc[...] = jnp.zeros(m_sc.shape, F32)

    t_idx = lax.broadcasted_iota(jnp.int32, (L, L), 0)
    s_idx = lax.broadcasted_iota(jnp.int32, (L, L), 1)
    mask = (s_idx >= t_idx) if rev else (s_idx <= t_idx)
    tri = mask.astype(F32)
    gc = gc_ref[...]
    gr = gr_ref[...]
    b_col = jnp.dot(tri, _log_sigmoid(gc), preferred_element_type=F32, precision=lax.Precision.HIGHEST)
    b_row = lax.dot_general(_log_sigmoid(gr), tri, (((1,), (1,)), ((), ())),
                            preferred_element_type=F32, precision=lax.Precision.HIGHEST)
    last = 0 if rev else L - 1
    i_base, f_base = (2 * ML_HEADS, 3 * ML_HEADS) if rev else (0, ML_HEADS)
    lane_e = lax.broadcasted_iota(jnp.int32, (L, LANES), 1)
    ones_col = jnp.where(lane_e == 0, 1.0, 0.0).astype(BF16)

    for hh in range(ML_HEADS):
        ci, cf = i_base + hh, f_base + hh
        bc = b_col[:, cf:cf + 1]
        br = b_row[cf:cf + 1, :]
        lic = gc[:, ci:ci + 1]
        lir = gr[ci:ci + 1, :]
        g = bc[last:last + 1, :]
        m_prev = m_sc[hh][0:1, 0:1]
        q = q_ref[:, hh * dk:(hh + 1) * dk].astype(BF16)
        kf = k_ref[:, hh * dk:(hh + 1) * dk] * (dk ** -0.5)
        v_ext = jnp.concatenate([v_ref[:, hh * dvv:(hh + 1) * dvv].astype(BF16), ones_col], axis=-1)

        d_log = jnp.where(mask, bc - br + lir, NEG_INF)
        inter = bc + m_prev
        m_t = jnp.maximum(inter, jnp.max(d_log, axis=-1, keepdims=True))
        w_intra = jnp.exp(d_log - m_t)
        w_inter = jnp.exp(inter - m_t)
        s = lax.dot_general(q, kf.astype(BF16), (((1,), (1,)), ((), ())), preferred_element_type=F32) * w_intra
        c_old = c_sc[hh]
        ext = w_inter * jnp.dot(q, c_old.astype(BF16), preferred_element_type=F32) \
            + jnp.dot(s.astype(BF16), v_ext, preferred_element_type=F32)
        den = ext[:, dvv:dvv + 1]
        h_ref[:, hh * dvv:(hh + 1) * dvv] = ext[:, :dvv] / jnp.maximum(jnp.abs(den), jnp.exp(-m_t))

        a = g - bc + lic
        m_new = jnp.maximum(g + m_prev, jnp.max(a, axis=0, keepdims=True))
        w_s = jnp.exp(a - m_new)
        decay = jnp.exp(g + m_prev - m_new)
        kw = (kf * w_s).astype(BF16)
        c_sc[hh] = decay * c_old + lax.dot_general(kw, v_ext, (((0,), (0,)), ((), ())),
                                                   preferred_element_type=F32)
        m_sc[hh] = jnp.broadcast_to(m_new, m_sc.shape[1:])


def _mlstm_scan(qkvo, gates, gates_t, n_lat, n_ctx, rev):
    t = n_lat + n_ctx
    L = ML_L
    nlat, nctx = n_lat // L, n_ctx // L
    nqk, nv = ML_HEADS * ML_QK_DIM, ML_HEADS * ML_V_DIM

    def chunk(c):
        if rev:
            return nlat + nctx - 1 - c
        return jnp.where(c < nctx, nlat + c, c - nctx)

    return pl.pallas_call(
        functools.partial(_mlstm_kernel, rev=rev),
        grid=(nlat + nctx,),
        in_specs=[pl.BlockSpec((L, nqk), lambda c: (chunk(c), 0)),
                  pl.BlockSpec((L, nqk), lambda c: (chunk(c), 1)),
                  pl.BlockSpec((L, nv), lambda c: (chunk(c), (2 * nqk) // nv)),
                  pl.BlockSpec((L, LANES), lambda c: (chunk(c), 0)),
                  pl.BlockSpec((4 * ML_HEADS, L), lambda c: (0, chunk(c)))],
        out_specs=pl.BlockSpec((L, nv), lambda c: (chunk(c), 0)),
        out_shape=jax.ShapeDtypeStruct((t, nv), F32),
        scratch_shapes=[pltpu.VMEM((ML_HEADS, ML_QK_DIM, ML_EXT), F32),
                        pltpu.VMEM((ML_HEADS, 8, LANES), F32)],
        compiler_params=_params("arbitrary"),
        name="mlstm_rev" if rev else "mlstm_fwd",
    )(qkvo, qkvo, qkvo, gates, gates_t)


def _mlstm_finish_kernel(hf_ref, hb_ref, o_ref, g_ref, y_ref):
    dvv = ML_V_DIM
    for hh in range(ML_HEADS):
        sl = slice(hh * dvv, (hh + 1) * dvv)
        h = hf_ref[:, sl] + hb_ref[:, sl]
        hn = h * lax.rsqrt(jnp.mean(h * h, axis=-1, keepdims=True) + EPS) * g_ref[:, sl]
        y_ref[:, sl] = (hn * jax.nn.sigmoid(o_ref[:, sl])).astype(BF16)


def _mlstm_finish(h_f, h_b, qkvo, h_gain):
    t, nv = h_f.shape
    tm = ROW_TILE
    o_block = (2 * ML_HEADS * ML_QK_DIM + nv) // nv
    return pl.pallas_call(
        _mlstm_finish_kernel,
        grid=(t // tm,),
        in_specs=[pl.BlockSpec((tm, nv), lambda i: (i, 0)),
                  pl.BlockSpec((tm, nv), lambda i: (i, 0)),
                  pl.BlockSpec((tm, nv), lambda i: (i, o_block)),
                  pl.BlockSpec((1, nv), lambda i: (0, 0))],
        out_specs=pl.BlockSpec((tm, nv), lambda i: (i, 0)),
        out_shape=jax.ShapeDtypeStruct((t, nv), BF16),
        compiler_params=_params("parallel"),
        name="mlstm_finish",
    )(h_f, h_b, qkvo, h_gain.reshape(1, nv))


def _mlstm_mixer(h, w_in_all, layer, gate_b, h_gain, n_lat, n_ctx):
    n_main = 2 * ML_HEADS * ML_QK_DIM + 2 * ML_HEADS * ML_V_DIM
    n_g = 4 * ML_HEADS
    qkvo = _mm(h, w_in_all, F32, tn=512, n_out=n_main, layer=layer)
    w_g = jnp.pad(w_in_all[layer, :, n_main:], ((0, 0), (0, LANES - n_g))).astype(BF16)
    b_g = jnp.pad(gate_b.astype(F32), (0, LANES - n_g)).reshape(1, LANES)
    gates = _mm(h, w_g, F32, tn=LANES, bias=b_g)
    gates_t = gates[:, :n_g].T
    h_f = _mlstm_scan(qkvo, gates, gates_t, n_lat, n_ctx, False)
    h_b = _mlstm_scan(qkvo, gates, gates_t, n_lat, n_ctx, True)
    return _mlstm_finish(h_f, h_b, qkvo, h_gain)


def _rope_tables(n_lat, n_ctx, rot_dim):
    rows = n_lat // GRID_W
    row = jnp.repeat(jnp.arange(rows, dtype=F32), GRID_W)
    col = jnp.tile(jnp.arange(GRID_W, dtype=F32), rows)
    n_freq = rot_dim // 4
    inv_freq = ROPE_THETA ** (-jnp.arange(n_freq, dtype=F32) / n_freq)
    ang = jnp.concatenate([row[:, None] * inv_freq, col[:, None] * inv_freq], axis=-1)
    cos, sin = jnp.cos(ang), jnp.sin(ang)
    reps = LANES // rot_dim
    cos_t = jnp.tile(jnp.concatenate([cos, cos], axis=-1), (1, reps))
    sin_t = jnp.tile(jnp.concatenate([-sin, sin], axis=-1), (1, reps))
    cos_t = jnp.concatenate([cos_t, jnp.ones((n_ctx, LANES), F32)], axis=0)
    sin_t = jnp.concatenate([sin_t, jnp.zeros((n_ctx, LANES), F32)], axis=0)
    return cos_t, sin_t


def kernel(x, c, ctx, c_ctx, ada_down, ada_up, ada_b, norm_mix, norm_ffn, ffn_in, ffn_out, att_in, att_qnorm,
           att_knorm, att_out, ml_in, ml_gate_b, ml_hnorm, ml_out, mla_in, mla_qnorm, mla_kvnorm, mla_qb,
           mla_kvb, mla_out, final_norm):
    b, n_lat, d = x.shape
    n_ctx = ctx.shape[1]
    depth = ada_down.shape[0]
    assert b == 1 and d == D_MODEL and n_lat % ROW_TILE == 0 and n_ctx % ML_L == 0 and n_lat % ML_L == 0

    cos_a, sin_a = _rope_tables(n_lat, n_ctx, ATT_HEAD_DIM)
    cos_m, sin_m = _rope_tables(n_lat, n_ctx, MLA_ROPE)

    cond = jnp.concatenate([c.reshape(1, d), c_ctx.reshape(1, d), jnp.zeros((6, d), F32)], axis=0)
    mods = _ada_all(cond, ada_down, ada_up, ada_b).reshape(depth, 8, N_MOD, d)

    xs = jnp.concatenate([x[0], ctx[0]], axis=0)
    t = n_lat + n_ctx
    tm_out = _pick(t, (1280, 1024, 640, 512, 256))
    tm_ffn = _pick(t, (640, 512, 256))
    ffn_out_bf16 = ffn_out.astype(BF16)

    for i in range(depth):
        kind, j = i % 3, i // 3
        mod = mods[i, :2]
        h = _rms_rows(xs, norm_mix[i], BF16, mod=mod, ks=0, kc=1, n_lat=n_lat)
        if kind == 0:
            o = _gqa_mixer(h, att_in, j, att_qnorm[j], att_knorm[j], cos_a, sin_a, n_lat, n_ctx)
            w_o = att_out
        elif kind == 1:
            o = _mlstm_mixer(h, ml_in, j, ml_gate_b[j], ml_hnorm[j], n_lat, n_ctx)
            w_o = ml_out
        else:
            o = _mla_mixer(h, mla_in[j], mla_qnorm[j], mla_kvnorm[j], mla_qb[j], mla_kvb[j], cos_m, sin_m,
                           n_lat, n_ctx)
            w_o = mla_out
        xs = _mm_res(o, w_o.astype(BF16), xs, mod[:, 2], n_lat, tm=tm_out, tn=512, tk=w_o.shape[1], layer=j)
        h2 = _rms_rows(xs, norm_ffn[i], BF16, mod=mod, ks=3, kc=4, n_lat=n_lat)
        act = _mm_swiglu(h2, ffn_in, tn=256, layer=i)
        xs = _mm_res(act, ffn_out_bf16, xs, mod[:, 5], n_lat, tm=tm_ffn, tn=1024, tk=D_FF // 2, layer=i)

    out = _rms_rows(xs, final_norm, F32, rows=n_lat)
    return out.reshape(1, n_lat, d)
```
